```python
import jax, jax.numpy as jnp
from jax import lax
import numpy as np

D_MODEL = 1024
BATCH = 8
SEQ = 4096
DEPTH = 1

HEAD_DIM = 64
ATT_WIDTH = D_MODEL // 2
N_ATT_HEADS = ATT_WIDTH // HEAD_DIM
DIL_PATTERNS = ((128, 1), (512, 4), (2048, 16))
POOL_WIDTH = D_MODEL - ATT_WIDTH
POOL_WINDOWS = (2, 4, 8, 16)
N_POOL_GROUPS = len(POOL_WINDOWS)
POOL_GROUP_DIM = POOL_WIDTH // N_POOL_GROUPS
IN_WIDTH = 3 * ATT_WIDTH + POOL_WIDTH
MIX_WIDTH = ATT_WIDTH + POOL_WIDTH
D_FF = ((8 * D_MODEL // 3 + 127) // 128) * 128
CONV_WIDTH = 3
PLE_DIM = 256
EPS = 1e-6

kernel_name = "hybrid_dilated_attn_multiscale_pool_block"


def _rmsnorm(t, g):
    tf = t.astype(jnp.float32)
    inv = lax.rsqrt(jnp.mean(tf * tf, axis=-1, keepdims=True) + EPS)
    return (tf * inv * g.astype(jnp.float32)).astype(t.dtype)


def _alibi_slopes(n_heads):
    return jnp.exp2(-8.0 * (jnp.arange(n_heads, dtype=jnp.float32) + 1.0) / n_heads)


def _dilated_branch(q, k, v, slopes, window, dil):
    B, S, H, Dh = q.shape
    span = window // dil
    unit = span * dil
    Sp = -(-S // unit) * unit
    L = Sp // dil
    nb = L // span
    pad = ((0, 0), (0, Sp - S), (0, 0), (0, 0))

    def blocks(t):
        return jnp.pad(t, pad).reshape(B, nb, span, dil, H, Dh)

    def with_prev(t):
        prev = jnp.pad(t, ((0, 0), (1, 0), (0, 0), (0, 0), (0, 0), (0, 0)))[:, :nb]
        return jnp.concatenate([prev, t], axis=2)

    qb = blocks(q)
    kk = with_prev(blocks(k))
    vv = with_prev(blocks(v))
    s = jnp.einsum('bnqrhd,bnkrhd->bnrhqk', qb, kk,
                   preferred_element_type=jnp.float32)
    qi = jnp.arange(span)[:, None]
    kj = jnp.arange(2 * span)[None, :]
    diff = qi + span - kj
    blk = jnp.arange(nb)[:, None, None]
    valid = (diff >= 0) & (diff <= span) & (blk * span - span + kj[None] >= 0)
    dist = (diff * dil).astype(jnp.float32)
    s = s - slopes[:, None, None] * dist
    s = jnp.where(valid[:, None, None], s, -jnp.inf)
    m = jnp.max(s, axis=-1, keepdims=True)
    e = jnp.exp(s - m)
    den = jnp.sum(e, axis=-1)
    o = jnp.einsum('bnrhqk,bnkrhd->bnqrhd', e, vv.astype(jnp.float32))
    den_t = jnp.transpose(den, (0, 1, 4, 2, 3))
    m_t = jnp.transpose(m[..., 0], (0, 1, 4, 2, 3))
    o = o / den_t[..., None]
    o = o.reshape(B, Sp, H, Dh)[:, :S]
    return o, m_t.reshape(B, Sp, H)[:, :S], den_t.reshape(B, Sp, H)[:, :S]


def _dilated_attention(q, k, v, slopes):
    outs, maxes, dens = [], [], []
    for window, dil in DIL_PATTERNS:
        o, m, d = _dilated_branch(q, k, v, slopes, window, dil)
        outs.append(o)
        maxes.append(m)
        dens.append(d)
    o = jnp.stack(outs)
    m = jnp.stack(maxes)
    d = jnp.stack(dens)
    w = d * jnp.exp(m - jnp.max(m, axis=0, keepdims=True))
    w = w / jnp.sum(w, axis=0, keepdims=True)
    return jnp.sum(w[..., None] * o, axis=0)


def _multiscale_pool(u, pool_w, pool_scale):
    B, S, _ = u.shape
    ug = u.astype(jnp.float32).reshape(B, S, N_POOL_GROUPS, POOL_GROUP_DIM)
    cs0 = jnp.pad(jnp.cumsum(ug, axis=1), ((0, 0), (1, 0), (0, 0), (0, 0)))
    upper = cs0[:, 1:]
    t = jnp.arange(S)
    diffs = []
    for g, w in enumerate(POOL_WINDOWS):
        lower = jnp.pad(cs0[:, :, g], ((0, 0), (w - 1, 0), (0, 0)))[:, :S]
        count = jnp.minimum(t + 1, w).astype(jnp.float32)[None, :, None]
        diffs.append((upper[:, :, g] - lower) / count - ug[:, :, g])
    dlt = jnp.stack(diffs, axis=2).astype(u.dtype)
    y = jnp.einsum('bsgc,gce->bsge', dlt, pool_w)
    y = y * pool_scale.reshape(N_POOL_GROUPS, POOL_GROUP_DIM)
    return y.reshape(B, S, POOL_WIDTH)


def _shift(t, s):
    if s == 0:
        return t
    return jnp.pad(t, ((0, 0), (s, 0), (0, 0)))[:, :t.shape[1]]


def _causal_dwconv(t, w, b):
    y = b
    for kk in range(CONV_WIDTH):
        y = y + w[kk] * _shift(t, CONV_WIDTH - 1 - kk)
    return y


def setup_inputs(seed: int = 0) -> dict:
    key = jax.random.key(seed)
    ks = jax.random.split(key, 20)
    f32 = jnp.float32

    def nrm(k, shape, scale):
        return jax.random.normal(k, shape, f32) * scale

    return {
        "x": nrm(ks[0], (BATCH, SEQ, D_MODEL), 1.0),
        "p": nrm(ks[1], (DEPTH, BATCH, SEQ, PLE_DIM), 1.0),
        "ln_mix": 1.0 + nrm(ks[2], (DEPTH, D_MODEL), 0.02),
        "w_in": nrm(ks[3], (DEPTH, D_MODEL, IN_WIDTH), D_MODEL ** -0.5),
        "pool_w": nrm(ks[4], (DEPTH, N_POOL_GROUPS, POOL_GROUP_DIM, POOL_GROUP_DIM), POOL_GROUP_DIM ** -0.5),
        "pool_scale": 1.0 + nrm(ks[5], (DEPTH, POOL_WIDTH), 0.1),
        "w_out": nrm(ks[6], (DEPTH, MIX_WIDTH, D_MODEL), MIX_WIDTH ** -0.5),
        "ln_ffn": 1.0 + nrm(ks[7], (DEPTH, D_MODEL), 0.02),
        "w_up": nrm(ks[8], (DEPTH, D_MODEL, 2 * D_FF), D_MODEL ** -0.5),
        "conv_w": nrm(ks[9], (DEPTH, CONV_WIDTH, 2 * D_FF), CONV_WIDTH ** -0.5),
        "conv_b": nrm(ks[10], (DEPTH, 2 * D_FF), 0.02),
        "w_down": nrm(ks[11], (DEPTH, D_FF, D_MODEL), D_FF ** -0.5),
        "ln_ple": 1.0 + nrm(ks[12], (DEPTH, D_MODEL), 0.02),
        "w_ple_gate": nrm(ks[13], (DEPTH, D_MODEL, D_MODEL), D_MODEL ** -0.5),
        "w_ple": nrm(ks[14], (DEPTH, PLE_DIM, D_MODEL), PLE_DIM ** -0.5),
        "ln_final": 1.0 + nrm(ks[15], (D_MODEL,), 0.02),
    }


def reference(x, p, ln_mix, w_in, pool_w, pool_scale, w_out, ln_ffn, w_up, conv_w, conv_b,
              w_down, ln_ple, w_ple_gate, w_ple, ln_final):
    B, S, _ = x.shape
    slopes = _alibi_slopes(N_ATT_HEADS)
    h = x
    for i in range(DEPTH):
        hn = _rmsnorm(h, ln_mix[i])
        z = hn @ w_in[i]
        q = z[..., :ATT_WIDTH].reshape(B, S, N_ATT_HEADS, HEAD_DIM) * (HEAD_DIM ** -0.5)
        k = z[..., ATT_WIDTH:2 * ATT_WIDTH].reshape(B, S, N_ATT_HEADS, HEAD_DIM)
        v = z[..., 2 * ATT_WIDTH:3 * ATT_WIDTH].reshape(B, S, N_ATT_HEADS, HEAD_DIM)
        u = z[..., 3 * ATT_WIDTH:]
        att = _dilated_attention(q, k, v, slopes).reshape(B, S, ATT_WIDTH).astype(h.dtype)
        pool = _multiscale_pool(u, pool_w[i], pool_scale[i]).astype(h.dtype)
        h = h + jnp.concatenate([att, pool], axis=-1) @ w_out[i]
        hn = _rmsnorm(h, ln_ffn[i])
        up = _causal_dwconv(hn @ w_up[i], conv_w[i], conv_b[i])
        gate, val = jnp.split(up, 2, axis=-1)
        h = h + (jax.nn.silu(gate) * val) @ w_down[i]
        g = jax.nn.sigmoid(_rmsnorm(h, ln_ple[i]) @ w_ple_gate[i])
        h = h + g * (p[i] @ w_ple[i])
    return _rmsnorm(h, ln_final)
```

```python
import functools

import jax
import jax.numpy as jnp
from jax import lax
from jax.experimental import pallas as pl
from jax.experimental.pallas import tpu as pltpu

D_MODEL = 1024
HEAD_DIM = 64
ATT_WIDTH = 512
N_ATT_HEADS = ATT_WIDTH // HEAD_DIM
DIL_PATTERNS = ((128, 1), (512, 4), (2048, 16))
SPAN = 128
POOL_WIDTH = 512
POOL_WINDOWS = (2, 4, 8, 16)
POOL_GROUP_DIM = 128
MAX_POOL_WINDOW = 16
IN_WIDTH = 3 * ATT_WIDTH + POOL_WIDTH
D_FF = 2816
CONV_WIDTH = 3
PLE_DIM = 256
EPS = 1e-6

LANES = 128
SUBLANES = 8
HEADS_PER_SLAB = LANES // HEAD_DIM
VMEM_LIMIT_BYTES = 56 * 1024 * 1024

F32 = jnp.float32
BF16 = jnp.bfloat16


def _rms_scale(t, g):
    inv = lax.rsqrt(jnp.mean(t * t, axis=-1, keepdims=True) + EPS)
    return t * inv * g


def _in_proj_body(x_ref, g_ref, w_ref, pw_ref, ps_ref, q_ref, k_ref, v_ref, pool_ref, ubuf, *, tm):
    s_idx = pl.program_id(1)
    hn = _rms_scale(x_ref[...], g_ref[...]).astype(BF16)
    a = ATT_WIDTH
    q_ref[...] = jnp.dot(hn, w_ref[:, 0:a], preferred_element_type=F32) * (HEAD_DIM ** -0.5)
    k_ref[...] = jnp.dot(hn, w_ref[:, a:2 * a], preferred_element_type=F32)
    v_ref[...] = jnp.dot(hn, w_ref[:, 2 * a:3 * a], preferred_element_type=F32)
    u = jnp.dot(hn, w_ref[:, 3 * a:], preferred_element_type=F32)

    hw = MAX_POOL_WINDOW

    @pl.when(s_idx == 0)
    def _():
        ubuf[0:hw, :] = jnp.zeros((hw, POOL_WIDTH), F32)

    @pl.when(s_idx != 0)
    def _():
        ubuf[0:hw, :] = ubuf[tm:tm + hw, :]

    ubuf[hw:, :] = u
    e = ubuf[...]
    gd = POOL_GROUP_DIM
    s2 = e + pltpu.roll(e, 1, 0)
    s4 = s2[:, gd:] + pltpu.roll(s2[:, gd:], 2, 0)
    s8 = s4[:, gd:] + pltpu.roll(s4[:, gd:], 4, 0)
    s16 = s8[:, gd:] + pltpu.roll(s8[:, gd:], 8, 0)
    sums = (s2[hw:, 0:gd], s4[hw:, 0:gd], s8[hw:, 0:gd], s16[hw:, :])
    t1 = (s_idx * tm + lax.broadcasted_iota(jnp.int32, (tm, 1), 0) + 1).astype(F32)
    for g, w in enumerate(POOL_WINDOWS):
        inv_cnt = 1.0 / jnp.minimum(t1, float(w))
        dlt = sums[g] * inv_cnt - u[:, g * gd:(g + 1) * gd]
        y = jnp.dot(dlt.astype(BF16), pw_ref[g], preferred_element_type=F32)
        pool_ref[:, g * gd:(g + 1) * gd] = (y * ps_ref[:, g * gd:(g + 1) * gd]).astype(BF16)


def _in_proj(x, ln_mix, w_in, pool_w, pool_scale, *, tm):
    B, S, D = x.shape
    grid = (B, S // tm)
    row = lambda b, s: (b, s, 0)
    const2 = lambda b, s: (0, 0)
    const3 = lambda b, s: (0, 0, 0)
    slab = pl.BlockSpec((None, tm, ATT_WIDTH), row)
    return pl.pallas_call(
        functools.partial(_in_proj_body, tm=tm),
        grid=grid,
        in_specs=[
            pl.BlockSpec((None, tm, D), row),
            pl.BlockSpec((1, D), const2),
            pl.BlockSpec((D, IN_WIDTH), const2),
            pl.BlockSpec((len(POOL_WINDOWS), POOL_GROUP_DIM, POOL_GROUP_DIM), const3),
            pl.BlockSpec((1, POOL_WIDTH), const2),
        ],
        out_specs=[slab, slab, slab, slab],
        out_shape=[
            jax.ShapeDtypeStruct((B, S, ATT_WIDTH), F32),
            jax.ShapeDtypeStruct((B, S, ATT_WIDTH), F32),
            jax.ShapeDtypeStruct((B, S, ATT_WIDTH), F32),
            jax.ShapeDtypeStruct((B, S, POOL_WIDTH), BF16),
        ],
        scratch_shapes=[pltpu.VMEM((MAX_POOL_WINDOW + tm, POOL_WIDTH), F32)],
        compiler_params=pltpu.CompilerParams(
            dimension_semantics=("arbitrary", "arbitrary"),
            vmem_limit_bytes=VMEM_LIMIT_BYTES),
        name="in_proj_pool",
    )(x, ln_mix.reshape(1, D), w_in.astype(BF16), pool_w.astype(BF16), pool_scale.reshape(1, POOL_WIDTH))


LAYOUT_CHUNK = 256


def _attn_body(slopes_ref, q_ref, k_ref, v_ref, o_ref, qh_s, k_s, v_s, bias_s, m_s, d_s, u_s, *, seq):
    hp = pl.program_id(1)
    n_blocks = seq // SPAN
    lane = lax.broadcasted_iota(jnp.int32, (1, LANES), 1)
    is_h0 = lane < HEAD_DIM

    qi = lax.broadcasted_iota(jnp.int32, (SPAN, 2 * SPAN), 0)
    kj = lax.broadcasted_iota(jnp.int32, (SPAN, 2 * SPAN), 1)
    diff = qi + SPAN - kj
    in_window = (diff >= 0) & (diff <= SPAN)
    diff_f = diff.astype(F32)

    for p, (_, dil) in enumerate(DIL_PATTERNS):
        k_s[p, 0:SPAN, :] = jnp.zeros((SPAN, LANES), BF16)
        v_s[p, 0:SPAN, 0:LANES] = jnp.zeros((SPAN, LANES), BF16)
        v_s[p, :, LANES:] = jnp.ones((SPAN + seq, LANES), BF16)

        for hh in range(HEADS_PER_SLAB):
            slope = slopes_ref[hp * HEADS_PER_SLAB + hh]
            penalty = (-(slope * float(dil))) * diff_f
            bias_s[p, 0, hh] = jnp.where(in_window & (kj >= SPAN), penalty, -jnp.inf)
            bias_s[p, 1, hh] = jnp.where(in_window, penalty, -jnp.inf)

        class_chunks = (seq // dil) // LAYOUT_CHUNK

        def regroup(c, carry, p=p, dil=dil, class_chunks=class_chunks):
            cls = c // class_chunks
            j = c % class_chunks
            start = cls + (dil * LAYOUT_CHUNK) * j
            if dil == 1:
                src = pl.ds(pl.multiple_of(start, LAYOUT_CHUNK), LAYOUT_CHUNK)
            else:
                src = pl.ds(start, LAYOUT_CHUNK, stride=dil)
            dst0 = pl.multiple_of(c * LAYOUT_CHUNK, LAYOUT_CHUNK)
            q = q_ref[src, :]
            qh_s[p, 0, pl.ds(dst0, LAYOUT_CHUNK), :] = jnp.where(is_h0, q, 0.0).astype(BF16)
            qh_s[p, 1, pl.ds(dst0, LAYOUT_CHUNK), :] = jnp.where(is_h0, 0.0, q).astype(BF16)
            dst1 = pl.multiple_of(SPAN + c * LAYOUT_CHUNK, SPAN)
            k_s[p, pl.ds(dst1, LAYOUT_CHUNK), :] = k_ref[src, :].astype(BF16)
            v_s[p, pl.ds(dst1, LAYOUT_CHUNK), 0:LANES] = v_ref[src, :].astype(BF16)
            return carry

        lax.fori_loop(0, seq // LAYOUT_CHUNK, regroup, 0)

    for p in (2, 1, 0):
        dil = DIL_PATTERNS[p][1]
        blocks_per_class = n_blocks // dil

        def block(g, carry, p=p, dil=dil, blocks_per_class=blocks_per_class):
            cls = g // blocks_per_class
            jb = g % blocks_per_class
            variant = jnp.where(jb == 0, 0, 1)
            row0 = pl.multiple_of(g * SPAN, SPAN)
            kw = k_s[p, pl.ds(row0, 2 * SPAN), :]
            vw = v_s[p, pl.ds(row0, 2 * SPAN), :]
            ms, os_ = [], []
            for hh in range(HEADS_PER_SLAB):
                qb = qh_s[p, hh, pl.ds(row0, SPAN), :]
                s = lax.dot_general(qb, kw, (((1,), (1,)), ((), ())), preferred_element_type=F32)
                s = s + bias_s[p, variant, hh]
                m = jnp.max(s, axis=1, keepdims=True)
                e = jnp.exp(s - m).astype(BF16)
                ms.append(m)
                os_.append(jnp.dot(e, vw, preferred_element_type=F32))
            m_new = jnp.where(is_h0, ms[0], ms[1])
            u_new = jnp.where(is_h0, os_[0][:, 0:LANES], os_[1][:, 0:LANES])
            d_new = jnp.where(is_h0, os_[0][:, LANES:], os_[1][:, LANES:])
            start = (dil * SPAN) * jb + cls
            if dil == 1:
                rows = pl.ds(pl.multiple_of(start, SPAN), SPAN)
            else:
                rows = pl.ds(start, SPAN, stride=dil)
            if p == 2:
                m_s[rows, :] = m_new
                d_s[rows, :] = d_new
                u_s[rows, :] = u_new
            else:
                m_old = m_s[rows, :]
                m_all = jnp.maximum(m_old, m_new)
                w_old = jnp.exp(m_old - m_all)
                w_new = jnp.exp(m_new - m_all)
                d_all = d_s[rows, :] * w_old + d_new * w_new
                u_all = u_s[rows, :] * w_old + u_new * w_new
                if p == 1:
                    m_s[rows, :] = m_all
                    d_s[rows, :] = d_all
                    u_s[rows, :] = u_all
                else:
                    o_ref[rows, :] = (u_all / d_all).astype(o_ref.dtype)
            return carry

        lax.fori_loop(0, n_blocks, block, 0)


def _attention(q, k, v, slopes):
    B, S, _ = q.shape
    n_slabs = ATT_WIDTH // LANES
    slab = pl.BlockSpec((None, S, LANES), lambda b, h, slopes_ref: (b, 0, h))
    n_br = len(DIL_PATTERNS)
    return pl.pallas_call(
        functools.partial(_attn_body, seq=S),
        grid_spec=pltpu.PrefetchScalarGridSpec(
            num_scalar_prefetch=1,
            grid=(B, n_slabs),
            in_specs=[slab, slab, slab],
            out_specs=slab,
            scratch_shapes=[
                pltpu.VMEM((n_br, HEADS_PER_SLAB, S, LANES), BF16),
                pltpu.VMEM((n_br, SPAN + S, LANES), BF16),
                pltpu.VMEM((n_br, SPAN + S, 2 * LANES), BF16),
                pltpu.VMEM((n_br, 2, HEADS_PER_SLAB, SPAN, 2 * SPAN), F32),
                pltpu.VMEM((S, LANES), F32),
                pltpu.VMEM((S, LANES), F32),
                pltpu.VMEM((S, LANES), F32),
            ],
        ),
        out_shape=jax.ShapeDtypeStruct((B, S, ATT_WIDTH), BF16),
        compiler_params=pltpu.CompilerParams(
            dimension_semantics=("arbitrary", "arbitrary"),
            vmem_limit_bytes=VMEM_LIMIT_BYTES),
        name="dilated_attention",
    )(slopes, q, k, v)


FF_CHUNK = 256


def _ffn_body(x_ref, att_ref, pool_ref, p_ref, wout_ref, lnf_ref, wup_ref, cw_ref, cb_ref, wdn_ref,
              lnp_ref, wg_ref, wple_ref, lnfin_ref, o_ref, act_s, carry_s, *, tm):
    s_idx = pl.program_id(1)
    h = x_ref[...]
    h = h + jnp.dot(att_ref[...], wout_ref[0:ATT_WIDTH, :], preferred_element_type=F32)
    h = h + jnp.dot(pool_ref[...], wout_ref[ATT_WIDTH:, :], preferred_element_type=F32)

    hn = _rms_scale(h, lnf_ref[...]).astype(BF16)

    @pl.when(s_idx == 0)
    def _():
        carry_s[...] = jnp.zeros_like(carry_s)

    def conv(c0):
        cols = slice(c0, c0 + FF_CHUNK)
        pre = jnp.dot(hn, wup_ref[:, cols], preferred_element_type=F32)
        prev = carry_s[:, cols]
        ext = jnp.concatenate([prev, pre], axis=0)
        sh1 = pltpu.roll(ext, 1, 0)[SUBLANES:]
        sh2 = pltpu.roll(ext, 2, 0)[SUBLANES:]
        carry_s[:, cols] = pre[tm - SUBLANES:, :]
        y = cb_ref[:, cols] + cw_ref[0:1, cols] * sh2
        y = y + cw_ref[1:2, cols] * sh1
        return y + cw_ref[2:3, cols] * pre

    for c in range(D_FF // FF_CHUNK):
        c0 = c * FF_CHUNK
        gate = conv(c0)
        val = conv(D_FF + c0)
        act_s[:, c0:c0 + FF_CHUNK] = (gate * jax.nn.sigmoid(gate) * val).astype(BF16)

    h = h + jnp.dot(act_s[...], wdn_ref[...], preferred_element_type=F32)

    hn = _rms_scale(h, lnp_ref[...]).astype(BF16)
    g = jax.nn.sigmoid(jnp.dot(hn, wg_ref[...], preferred_element_type=F32))
    emb = jnp.dot(p_ref[...].astype(BF16), wple_ref[...], preferred_element_type=F32)
    h = h + g * emb
    o_ref[...] = _rms_scale(h, lnfin_ref[...])


def _ffn(x, att, pool, p, w_out, ln_ffn, w_up, conv_w, conv_b, w_down, ln_ple, w_ple_gate, w_ple,
         ln_final, *, tm):
    B, S, D = x.shape
    grid = (B, S // tm)
    row = lambda b, s: (b, s, 0)
    const2 = lambda b, s: (0, 0)

    def resident(shape):
        return pl.BlockSpec(shape, const2, pipeline_mode=pl.Buffered(1))

    return pl.pallas_call(
        functools.partial(_ffn_body, tm=tm),
        grid=grid,
        in_specs=[
            pl.BlockSpec((None, tm, D), row),
            pl.BlockSpec((None, tm, ATT_WIDTH), row),
            pl.BlockSpec((None, tm, POOL_WIDTH), row),
            pl.BlockSpec((None, tm, PLE_DIM), row),
            resident((ATT_WIDTH + POOL_WIDTH, D)),
            resident((1, D)),
            resident((D, 2 * D_FF)),
            resident((CONV_WIDTH, 2 * D_FF)),
            resident((1, 2 * D_FF)),
            resident((D_FF, D)),
            resident((1, D)),
            resident((D, D)),
            resident((PLE_DIM, D)),
            resident((1, D)),
        ],
        out_specs=pl.BlockSpec((None, tm, D), row),
        out_shape=jax.ShapeDtypeStruct((B, S, D), F32),
        scratch_shapes=[
            pltpu.VMEM((tm, D_FF), BF16),
            pltpu.VMEM((SUBLANES, 2 * D_FF), F32),
        ],
        compiler_params=pltpu.CompilerParams(
            dimension_semantics=("arbitrary", "arbitrary"),
            vmem_limit_bytes=VMEM_LIMIT_BYTES),
        name="outproj_ffn_ple",
    )(x, att, pool, p, w_out.astype(BF16), ln_ffn.reshape(1, D), w_up.astype(BF16), conv_w,
      conv_b.reshape(1, 2 * D_FF), w_down.astype(BF16), ln_ple.reshape(1, D), w_ple_gate.astype(BF16),
      w_ple.astype(BF16), ln_final.reshape(1, D))


def kernel(x, p, ln_mix, w_in, pool_w, pool_scale, w_out, ln_ffn, w_up, conv_w, conv_b, w_down, ln_ple,
           w_ple_gate, w_ple, ln_final):
    depth = p.shape[0]
    slopes = jnp.exp2(-8.0 * (jnp.arange(N_ATT_HEADS, dtype=F32) + 1.0) / N_ATT_HEADS)
    assert depth == 1, "the final RMSNorm is fused into the last layer's kernel"
    i = 0
    q, k, v, pool = _in_proj(x, ln_mix[i], w_in[i], pool_w[i], pool_scale[i], tm=512)
    att = _attention(q, k, v, slopes)
    return _ffn(x, att, pool, p[i], w_out[i], ln_ffn[i], w_up[i], conv_w[i], conv_b[i], w_down[i],
                ln_ple[i], w_ple_gate[i], w_ple[i], ln_final, tm=512)
```

```python
import functools

import jax
import jax.numpy as jnp
from jax import lax
from jax.experimental import pallas as pl
from jax.experimental.pallas import tpu as pltpu

D_MODEL = 1024
HEAD_DIM = 64
ATT_WIDTH = 512
N_ATT_HEADS = ATT_WIDTH // HEAD_DIM
DIL_PATTERNS = ((128, 1), (512, 4), (2048, 16))
SPAN = 128
POOL_WIDTH = 512
POOL_WINDOWS = (2, 4, 8, 16)
POOL_GROUP_DIM = 128
MAX_POOL_WINDOW = 16
IN_WIDTH = 3 * ATT_WIDTH + POOL_WIDTH
D_FF = 2816
CONV_WIDTH = 3
PLE_DIM = 256
EPS = 1e-6

LANES = 128
SUBLANES = 8
HEADS_PER_SLAB = LANES // HEAD_DIM
VMEM_LIMIT_BYTES = 56 * 1024 * 1024

F32 = jnp.float32
BF16 = jnp.bfloat16


def _rms_scale(t, g):
    inv = lax.rsqrt(jnp.mean(t * t, axis=-1, keepdims=True) + EPS)
    return t * inv * g


def _in_proj_body(x_ref, g_ref, w_ref, pw_ref, ps_ref, q_ref, k_ref, v_ref, pool_ref, ubuf, *, tm):
    s_idx = pl.program_id(1)
    hn = _rms_scale(x_ref[...], g_ref[...]).astype(BF16)
    a = ATT_WIDTH
    q_ref[...] = jnp.dot(hn, w_ref[:, 0:a], preferred_element_type=F32) * (HEAD_DIM ** -0.5)
    k_ref[...] = jnp.dot(hn, w_ref[:, a:2 * a], preferred_element_type=F32)
    v_ref[...] = jnp.dot(hn, w_ref[:, 2 * a:3 * a], preferred_element_type=F32)
    u = jnp.dot(hn, w_ref[:, 3 * a:], preferred_element_type=F32)

    hw = MAX_POOL_WINDOW

    @pl.when(s_idx == 0)
    def _():
        ubuf[0:hw, :] = jnp.zeros((hw, POOL_WIDTH), F32)

    @pl.when(s_idx != 0)
    def _():
        ubuf[0:hw, :] = ubuf[tm:tm + hw, :]

    ubuf[hw:, :] = u
    e = ubuf[...]
    gd = POOL_GROUP_DIM
    s2 = e + pltpu.roll(e, 1, 0)
    s4 = s2[:, gd:] + pltpu.roll(s2[:, gd:], 2, 0)
    s8 = s4[:, gd:] + pltpu.roll(s4[:, gd:], 4, 0)
    s16 = s8[:, gd:] + pltpu.roll(s8[:, gd:], 8, 0)
    sums = (s2[hw:, 0:gd], s4[hw:, 0:gd], s8[hw:, 0:gd], s16[hw:, :])
    t1 = (s_idx * tm + lax.broadcasted_iota(jnp.int32, (tm, 1), 0) + 1).astype(F32)
    for g, w in enumerate(POOL_WINDOWS):
        inv_cnt = 1.0 / jnp.minimum(t1, float(w))
        dlt = sums[g] * inv_cnt - u[:, g * gd:(g + 1) * gd]
        y = jnp.dot(dlt.astype(BF16), pw_ref[g], preferred_element_type=F32)
        pool_ref[:, g * gd:(g + 1) * gd] = (y * ps_ref[:, g * gd:(g + 1) * gd]).astype(BF16)


def _in_proj(x, ln_mix, w_in, pool_w, pool_scale, *, tm):
    B, S, D = x.shape
    grid = (B, S // tm)
    row = lambda b, s: (b, s, 0)
    const2 = lambda b, s: (0, 0)
    const3 = lambda b, s: (0, 0, 0)
    slab = pl.BlockSpec((None, tm, ATT_WIDTH), row)
    return pl.pallas_call(
        functools.partial(_in_proj_body, tm=tm),
        grid=grid,
        in_specs=[
            pl.BlockSpec((None, tm, D), row),
            pl.BlockSpec((1, D), const2),
            pl.BlockSpec((D, IN_WIDTH), const2),
            pl.BlockSpec((len(POOL_WINDOWS), POOL_GROUP_DIM, POOL_GROUP_DIM), const3),
            pl.BlockSpec((1, POOL_WIDTH), const2),
        ],
        out_specs=[slab, slab, slab, slab],
        out_shape=[
            jax.ShapeDtypeStruct((B, S, ATT_WIDTH), F32),
            jax.ShapeDtypeStruct((B, S, ATT_WIDTH), F32),
            jax.ShapeDtypeStruct((B, S, ATT_WIDTH), F32),
            jax.ShapeDtypeStruct((B, S, POOL_WIDTH), BF16),
        ],
        scratch_shapes=[pltpu.VMEM((MAX_POOL_WINDOW + tm, POOL_WIDTH), F32)],
        compiler_params=pltpu.CompilerParams(
            dimension_semantics=("arbitrary", "arbitrary"),
            vmem_limit_bytes=VMEM_LIMIT_BYTES),
        name="in_proj_pool",
    )(x, ln_mix.reshape(1, D), w_in.astype(BF16), pool_w.astype(BF16), pool_scale.reshape(1, POOL_WIDTH))


LAYOUT_CHUNK = 256
BLOCK_UNROLL = 8


def _attn_body(slopes_ref, q_ref, k_ref, v_ref, o_ref, qh_s, k_s, v_s, bias_s, m_s, d_s, u_s, *, seq):
    hp = pl.program_id(1)
    n_blocks = seq // SPAN
    lane = lax.broadcasted_iota(jnp.int32, (1, LANES), 1)
    is_h0 = lane < HEAD_DIM

    qi = lax.broadcasted_iota(jnp.int32, (SPAN, 2 * SPAN), 0)
    kj = lax.broadcasted_iota(jnp.int32, (SPAN, 2 * SPAN), 1)
    diff = qi + SPAN - kj
    in_window = (diff >= 0) & (diff <= SPAN)
    diff_f = diff.astype(F32)

    for p, (_, dil) in enumerate(DIL_PATTERNS):
        k_s[p, 0:SPAN, :] = jnp.zeros((SPAN, LANES), BF16)
        v_s[p, 0:SPAN, 0:LANES] = jnp.zeros((SPAN, LANES), BF16)
        v_s[p, :, LANES:] = jnp.ones((SPAN + seq, LANES), BF16)

        for hh in range(HEADS_PER_SLAB):
            slope = slopes_ref[hp * HEADS_PER_SLAB + hh]
            penalty = (-(slope * float(dil))) * diff_f
            bias_s[p, 0, hh] = jnp.where(in_window & (kj >= SPAN), penalty, -jnp.inf)
            bias_s[p, 1, hh] = jnp.where(in_window, penalty, -jnp.inf)

        class_chunks = (seq // dil) // LAYOUT_CHUNK

        def regroup(c, carry, p=p, dil=dil, class_chunks=class_chunks):
            cls = c // class_chunks
            j = c % class_chunks
            start = cls + (dil * LAYOUT_CHUNK) * j
            if dil == 1:
                src = pl.ds(pl.multiple_of(start, LAYOUT_CHUNK), LAYOUT_CHUNK)
            else:
                src = pl.ds(start, LAYOUT_CHUNK, stride=dil)
            dst0 = pl.multiple_of(c * LAYOUT_CHUNK, LAYOUT_CHUNK)
            q = q_ref[src, :]
            qh_s[p, 0, pl.ds(dst0, LAYOUT_CHUNK), :] = jnp.where(is_h0, q, 0.0).astype(BF16)
            qh_s[p, 1, pl.ds(dst0, LAYOUT_CHUNK), :] = jnp.where(is_h0, 0.0, q).astype(BF16)
            dst1 = pl.multiple_of(SPAN + c * LAYOUT_CHUNK, SPAN)
            k_s[p, pl.ds(dst1, LAYOUT_CHUNK), :] = k_ref[src, :].astype(BF16)
            v_s[p, pl.ds(dst1, LAYOUT_CHUNK), 0:LANES] = v_ref[src, :].astype(BF16)
            return carry

        lax.fori_loop(0, seq // LAYOUT_CHUNK, regroup, 0)

    for p in (2, 1, 0):
        dil = DIL_PATTERNS[p][1]
        blocks_per_class = n_blocks // dil

        def block(g, carry, p=p, dil=dil, blocks_per_class=blocks_per_class):
            cls = g // blocks_per_class
            jb = g % blocks_per_class
            variant = jnp.where(jb == 0, 0, 1)
            row0 = pl.multiple_of(g * SPAN, SPAN)
            kw = k_s[p, pl.ds(row0, 2 * SPAN), :]
            vw = v_s[p, pl.ds(row0, 2 * SPAN), :]
            ms, os_ = [], []
            for hh in range(HEADS_PER_SLAB):
                qb = qh_s[p, hh, pl.ds(row0, SPAN), :]
                s = lax.dot_general(qb, kw, (((1,), (1,)), ((), ())), preferred_element_type=F32)
                s = s + bias_s[p, variant, hh]
                m = jnp.max(s, axis=1, keepdims=True)
                e = jnp.exp(s - m).astype(BF16)
                ms.append(m)
                os_.append(jnp.dot(e, vw, preferred_element_type=F32))
            m_new = jnp.where(is_h0, ms[0], ms[1])
            u_new = jnp.where(is_h0, os_[0][:, 0:LANES], os_[1][:, 0:LANES])
            d_new = jnp.where(is_h0, os_[0][:, LANES:], os_[1][:, LANES:])
            start = (dil * SPAN) * jb + cls
            if dil == 1:
                rows = pl.ds(pl.multiple_of(start, SPAN), SPAN)
            else:
                rows = pl.ds(start, SPAN, stride=dil)
            if p == 2:
                m_s[rows, :] = m_new
                d_s[rows, :] = d_new
                u_s[rows, :] = u_new
            else:
                m_old = m_s[rows, :]
                m_all = jnp.maximum(m_old, m_new)
                w_old = jnp.exp(m_old - m_all)
                w_new = jnp.exp(m_new - m_all)
                d_all = d_s[rows, :] * w_old + d_new * w_new
                u_all = u_s[rows, :] * w_old + u_new * w_new
                if p == 1:
                    m_s[rows, :] = m_all
                    d_s[rows, :] = d_all
                    u_s[rows, :] = u_all
                else:
                    o_ref[rows, :] = (u_all / d_all).astype(o_ref.dtype)
            return carry

        lax.fori_loop(0, n_blocks, block, 0, unroll=BLOCK_UNROLL)


def _attention(q, k, v, slopes):
    B, S, _ = q.shape
    n_slabs = ATT_WIDTH // LANES
    slab = pl.BlockSpec((None, S, LANES), lambda b, h, slopes_ref: (b, 0, h))
    n_br = len(DIL_PATTERNS)
    return pl.pallas_call(
        functools.partial(_attn_body, seq=S),
        grid_spec=pltpu.PrefetchScalarGridSpec(
            num_scalar_prefetch=1,
            grid=(B, n_slabs),
            in_specs=[slab, slab, slab],
            out_specs=slab,
            scratch_shapes=[
                pltpu.VMEM((n_br, HEADS_PER_SLAB, S, LANES), BF16),
                pltpu.VMEM((n_br, SPAN + S, LANES), BF16),
                pltpu.VMEM((n_br, SPAN + S, 2 * LANES), BF16),
                pltpu.VMEM((n_br, 2, HEADS_PER_SLAB, SPAN, 2 * SPAN), F32),
                pltpu.VMEM((S, LANES), F32),
                pltpu.VMEM((S, LANES), F32),
                pltpu.VMEM((S, LANES), F32),
            ],
        ),
        out_shape=jax.ShapeDtypeStruct((B, S, ATT_WIDTH), BF16),
        compiler_params=pltpu.CompilerParams(
            dimension_semantics=("arbitrary", "arbitrary"),
            vmem_limit_bytes=VMEM_LIMIT_BYTES),
        name="dilated_attention",
    )(slopes, q, k, v)


FF_CHUNK = 256


def _ffn_body(x_ref, att_ref, pool_ref, p_ref, wout_ref, lnf_ref, wup_ref, cw_ref, cb_ref, wdn_ref,
              lnp_ref, wg_ref, wple_ref, lnfin_ref, o_ref, act_s, carry_s, *, tm):
    s_idx = pl.program_id(1)
    h = x_ref[...]
    h = h + jnp.dot(att_ref[...], wout_ref[0:ATT_WIDTH, :], preferred_element_type=F32)
    h = h + jnp.dot(pool_ref[...], wout_ref[ATT_WIDTH:, :], preferred_element_type=F32)

    hn = _rms_scale(h, lnf_ref[...]).astype(BF16)

    @pl.when(s_idx == 0)
    def _():
        carry_s[...] = jnp.zeros_like(carry_s)

    def conv(c0):
        cols = slice(c0, c0 + FF_CHUNK)
        pre = jnp.dot(hn, wup_ref[:, cols], preferred_element_type=F32)
        prev = carry_s[:, cols]
        ext = jnp.concatenate([prev, pre], axis=0)
        sh1 = pltpu.roll(ext, 1, 0)[SUBLANES:]
        sh2 = pltpu.roll(ext, 2, 0)[SUBLANES:]
        carry_s[:, cols] = pre[tm - SUBLANES:, :]
        y = cb_ref[:, cols] + cw_ref[0:1, cols] * sh2
        y = y + cw_ref[1:2, cols] * sh1
        return y + cw_ref[2:3, cols] * pre

    for c in range(D_FF // FF_CHUNK):
        c0 = c * FF_CHUNK
        gate = conv(c0)
        val = conv(D_FF + c0)
        act_s[:, c0:c0 + FF_CHUNK] = (gate * jax.nn.sigmoid(gate) * val).astype(BF16)

    h = h + jnp.dot(act_s[...], wdn_ref[...], preferred_element_type=F32)

    hn = _rms_scale(h, lnp_ref[...]).astype(BF16)
    g = jax.nn.sigmoid(jnp.dot(hn, wg_ref[...], preferred_element_type=F32))
    emb = jnp.dot(p_ref[...].astype(BF16), wple_ref[...], preferred_element_type=F32)
    h = h + g * emb
    o_ref[...] = _rms_scale(h, lnfin_ref[...])


def _ffn(x, att, pool, p, w_out, ln_ffn, w_up, conv_w, conv_b, w_down, ln_ple, w_ple_gate, w_ple,
         ln_final, *, tm):
    B, S, D = x.shape
    grid = (B, S // tm)
    row = lambda b, s: (b, s, 0)
    const2 = lambda b, s: (0, 0)

    def resident(shape):
        return pl.BlockSpec(shape, const2, pipeline_mode=pl.Buffered(1))

    return pl.pallas_call(
        functools.partial(_ffn_body, tm=tm),
        grid=grid,
        in_specs=[
            pl.BlockSpec((None, tm, D), row),
            pl.BlockSpec((None, tm, ATT_WIDTH), row),
            pl.BlockSpec((None, tm, POOL_WIDTH), row),
            pl.BlockSpec((None, tm, PLE_DIM), row),
            resident((ATT_WIDTH + POOL_WIDTH, D)),
            resident((1, D)),
            resident((D, 2 * D_FF)),
            resident((CONV_WIDTH, 2 * D_FF)),
            resident((1, 2 * D_FF)),
            resident((D_FF, D)),
            resident((1, D)),
            resident((D, D)),
            resident((PLE_DIM, D)),
            resident((1, D)),
        ],
        out_specs=pl.BlockSpec((None, tm, D), row),
        out_shape=jax.ShapeDtypeStruct((B, S, D), F32),
        scratch_shapes=[
            pltpu.VMEM((tm, D_FF), BF16),
            pltpu.VMEM((SUBLANES, 2 * D_FF), F32),
        ],
        compiler_params=pltpu.CompilerParams(
            dimension_semantics=("arbitrary", "arbitrary"),
            vmem_limit_bytes=VMEM_LIMIT_BYTES),
        name="outproj_ffn_ple",
    )(x, att, pool, p, w_out.astype(BF16), ln_ffn.reshape(1, D), w_up.astype(BF16), conv_w,
      conv_b.reshape(1, 2 * D_FF), w_down.astype(BF16), ln_ple.reshape(1, D), w_ple_gate.astype(BF16),
      w_ple.astype(BF16), ln_final.reshape(1, D))


def kernel(x, p, ln_mix, w_in, pool_w, pool_scale, w_out, ln_ffn, w_up, conv_w, conv_b, w_down, ln_ple,
           w_ple_gate, w_ple, ln_final):
    depth = p.shape[0]
    slopes = jnp.exp2(-8.0 * (jnp.arange(N_ATT_HEADS, dtype=F32) + 1.0) / N_ATT_HEADS)
    assert depth == 1, "the final RMSNorm is fused into the last layer's kernel"
    i = 0
    q, k, v, pool = _in_proj(x, ln_mix[i], w_in[i], pool_w[i], pool_scale[i], tm=512)
    att = _attention(q, k, v, slopes)
    return _ffn(x, att, pool, p[i], w_out[i], ln_ffn[i], w_up[i], conv_w[i], conv_b[i], w_down[i],
                ln_ple[i], w_ple_gate[i], w_ple[i], ln_final, tm=512)
```

```python
import functools

import jax
import jax.numpy as jnp
from jax import lax
from jax.experimental import pallas as pl
from jax.experimental.pallas import tpu as pltpu

D_MODEL = 1024
HEAD_DIM = 64
ATT_WIDTH = 512
N_ATT_HEADS = ATT_WIDTH // HEAD_DIM
DIL_PATTERNS = ((128, 1), (512, 4), (2048, 16))
SPAN = 128
POOL_WIDTH = 512
POOL_WINDOWS = (2, 4, 8, 16)
POOL_GROUP_DIM = 128
MAX_POOL_WINDOW = 16
IN_WIDTH = 3 * ATT_WIDTH + POOL_WIDTH
D_FF = 2816
CONV_WIDTH = 3
PLE_DIM = 256
EPS = 1e-6

LANES = 128
SUBLANES = 8
HEADS_PER_SLAB = LANES // HEAD_DIM
VMEM_LIMIT_BYTES = 56 * 1024 * 1024

F32 = jnp.float32
BF16 = jnp.bfloat16


def _rms_scale(t, g):
    inv = lax.rsqrt(jnp.mean(t * t, axis=-1, keepdims=True) + EPS)
    return t * inv * g


QKV_WIDTH = 3 * ATT_WIDTH
LOG2E = 1.4426950408889634
Q_SCALE = (HEAD_DIM ** -0.5) * LOG2E
N_QKV_SLABS = QKV_WIDTH // LANES


def _in_proj_body(x_ref, g_ref, w_ref, pw_ref, ps_ref, nat_ref, r4_ref, r16_ref, pool_ref, zbuf, ubuf,
                  *, tm):
    s_idx = pl.program_id(1)
    hn = _rms_scale(x_ref[...], g_ref[...]).astype(BF16)
    a = ATT_WIDTH
    slabs_per_part = a // LANES
    for part in range(3):
        z = jnp.dot(hn, w_ref[:, part * a:(part + 1) * a], preferred_element_type=F32)
        if part == 0:
            z = z * Q_SCALE
        nat_ref[:, part * a:(part + 1) * a] = z.astype(BF16)
        for sl in range(slabs_per_part):
            zbuf[part * slabs_per_part + sl] = z[:, sl * LANES:(sl + 1) * LANES]
    u = jnp.dot(hn, w_ref[:, 3 * a:], preferred_element_type=F32)

    for j in range(N_QKV_SLABS):
        cols = slice(j * LANES, (j + 1) * LANES)
        for dil, out_ref in ((4, r4_ref), (16, r16_ref)):
            for r in range(dil):
                out_ref[r, :, cols] = zbuf[j, pl.ds(r, tm // dil, stride=dil), :].astype(BF16)

    hw = MAX_POOL_WINDOW

    @pl.when(s_idx == 0)
    def _():
        ubuf[0:hw, :] = jnp.zeros((hw, POOL_WIDTH), F32)

    @pl.when(s_idx != 0)
    def _():
        ubuf[0:hw, :] = ubuf[tm:tm + hw, :]

    ubuf[hw:, :] = u
    e = ubuf[...]
    gd = POOL_GROUP_DIM
    s2 = e + pltpu.roll(e, 1, 0)
    s4 = s2[:, gd:] + pltpu.roll(s2[:, gd:], 2, 0)
    s8 = s4[:, gd:] + pltpu.roll(s4[:, gd:], 4, 0)
    s16 = s8[:, gd:] + pltpu.roll(s8[:, gd:], 8, 0)
    sums = (s2[hw:, 0:gd], s4[hw:, 0:gd], s8[hw:, 0:gd], s16[hw:, :])
    t1 = (s_idx * tm + lax.broadcasted_iota(jnp.int32, (tm, 1), 0) + 1).astype(F32)
    for g, w in enumerate(POOL_WINDOWS):
        inv_cnt = 1.0 / jnp.minimum(t1, float(w))
        dlt = sums[g] * inv_cnt - u[:, g * gd:(g + 1) * gd]
        y = jnp.dot(dlt.astype(BF16), pw_ref[g], preferred_element_type=F32)
        pool_ref[:, g * gd:(g + 1) * gd] = (y * ps_ref[:, g * gd:(g + 1) * gd]).astype(BF16)


def _in_proj(x, ln_mix, w_in, pool_w, pool_scale, *, tm):
    B, S, D = x.shape
    grid = (B, S // tm)
    row = lambda b, s: (b, s, 0)
    cls_row = lambda b, s: (b, 0, s, 0)
    const2 = lambda b, s: (0, 0)
    const3 = lambda b, s: (0, 0, 0)
    nat, r4, r16, pool = pl.pallas_call(
        functools.partial(_in_proj_body, tm=tm),
        grid=grid,
        in_specs=[
            pl.BlockSpec((None, tm, D), row),
            pl.BlockSpec((1, D), const2),
            pl.BlockSpec((D, IN_WIDTH), const2),
            pl.BlockSpec((len(POOL_WINDOWS), POOL_GROUP_DIM, POOL_GROUP_DIM), const3),
            pl.BlockSpec((1, POOL_WIDTH), const2),
        ],
        out_specs=[
            pl.BlockSpec((None, tm, QKV_WIDTH), row),
            pl.BlockSpec((None, 4, tm // 4, QKV_WIDTH), cls_row),
            pl.BlockSpec((None, 16, tm // 16, QKV_WIDTH), cls_row),
            pl.BlockSpec((None, tm, POOL_WIDTH), row),
        ],
        out_shape=[
            jax.ShapeDtypeStruct((B, S, QKV_WIDTH), BF16),
            jax.ShapeDtypeStruct((B, 4, S // 4, QKV_WIDTH), BF16),
            jax.ShapeDtypeStruct((B, 16, S // 16, QKV_WIDTH), BF16),
            jax.ShapeDtypeStruct((B, S, POOL_WIDTH), BF16),
        ],
        scratch_shapes=[
            pltpu.VMEM((N_QKV_SLABS, tm, LANES), F32),
            pltpu.VMEM((MAX_POOL_WINDOW + tm, POOL_WIDTH), F32),
        ],
        compiler_params=pltpu.CompilerParams(
            dimension_semantics=("arbitrary", "arbitrary"),
            vmem_limit_bytes=VMEM_LIMIT_BYTES),
        name="in_proj_pool",
    )(x, ln_mix.reshape(1, D), w_in.astype(BF16), pool_w.astype(BF16), pool_scale.reshape(1, POOL_WIDTH))
    return nat, r4.reshape(B, S, QKV_WIDTH), r16.reshape(B, S, QKV_WIDTH), pool


BLOCK_UNROLL = 8
N_BIAS_VARIANTS = 3


def _attn_body(slopes_ref, qn_ref, kn_ref, vn_ref, q4_ref, k4_ref, v4_ref, q16_ref, k16_ref, v16_ref,
               o_ref, bias_s, m_s, d_s, u_s, *, seq):
    hp = pl.program_id(1)
    n_blocks = seq // SPAN
    lane = lax.broadcasted_iota(jnp.int32, (1, LANES), 1)
    is_h0 = lane < HEAD_DIM
    head_mask = (is_h0, jnp.logical_not(is_h0))
    head_sel = tuple(jnp.where(hm, 1.0, 0.0).astype(BF16) for hm in head_mask)
    ones_cols = tuple(jnp.broadcast_to(sel, (2 * SPAN, LANES)) for sel in head_sel)

    qi = lax.broadcasted_iota(jnp.int32, (SPAN, 2 * SPAN), 0)
    kj = lax.broadcasted_iota(jnp.int32, (SPAN, 2 * SPAN), 1)
    diffs = (qi - kj, qi + SPAN - kj, qi + SPAN - kj)
    extra = (None, kj >= SPAN, None)
    for p, (_, dil) in enumerate(DIL_PATTERNS):
        for hh in range(HEADS_PER_SLAB):
            slope = slopes_ref[hp * HEADS_PER_SLAB + hh]
            for var in range(N_BIAS_VARIANTS):
                diff = diffs[var]
                valid = (diff >= 0) & (diff <= SPAN)
                if extra[var] is not None:
                    valid = valid & extra[var]
                penalty = (-(slope * (float(dil) * LOG2E))) * diff.astype(F32)
                bias_s[p, var, hh] = jnp.where(valid, penalty, -jnp.inf)

    branch_refs = ((qn_ref, kn_ref, vn_ref), (q4_ref, k4_ref, v4_ref), (q16_ref, k16_ref, v16_ref))
    for p in (2, 1, 0):
        dil = DIL_PATTERNS[p][1]
        blocks_per_class = n_blocks // dil
        q_ref, k_ref, v_ref = branch_refs[p]

        def block(g, carry, p=p, dil=dil, blocks_per_class=blocks_per_class,
                  q_ref=q_ref, k_ref=k_ref, v_ref=v_ref):
            cls = g // blocks_per_class
            jb = g % blocks_per_class
            variant = jnp.where(g == 0, 0, jnp.where(jb == 0, 1, 2))
            row0 = pl.multiple_of(g * SPAN, SPAN)
            win0 = pl.multiple_of(jnp.maximum(g - 1, 0) * SPAN, SPAN)
            qb = q_ref[pl.ds(row0, SPAN), :]
            kw = k_ref[pl.ds(win0, 2 * SPAN), :]
            vw = v_ref[pl.ds(win0, 2 * SPAN), :]
            ms, es, vs = [], [], []
            for hh in range(HEADS_PER_SLAB):
                qh = qb * head_sel[hh]
                s = lax.dot_general(qh, kw, (((1,), (1,)), ((), ())), preferred_element_type=F32)
                s = s + bias_s[p, variant, hh]
                m = jnp.max(s, axis=1, keepdims=True)
                ms.append(m)
                es.append(jnp.exp2(s - m).astype(BF16))
                vs.append(jnp.concatenate(
                    [jnp.where(head_mask[hh], vw, jnp.zeros_like(vw)), ones_cols[hh]], axis=1))
            ud = jnp.dot(jnp.concatenate(es, axis=1), jnp.concatenate(vs, axis=0),
                         preferred_element_type=F32)
            m_new = jnp.where(is_h0, ms[0], ms[1])
            u_new = ud[:, 0:LANES]
            d_new = ud[:, LANES:]
            start = (dil * SPAN) * jb + cls
            if dil == 1:
                rows = pl.ds(pl.multiple_of(start, SPAN), SPAN)
            else:
                rows = pl.ds(start, SPAN, stride=dil)
            if p == 2:
                m_s[rows, :] = m_new
                d_s[rows, :] = d_new
                u_s[rows, :] = u_new
            else:
                m_old = m_s[rows, :]
                m_all = jnp.maximum(m_old, m_new)
                w_old = jnp.exp2(m_old - m_all)
                w_new = jnp.exp2(m_new - m_all)
                d_all = d_s[rows, :] * w_old + d_new * w_new
                u_all = u_s[rows, :] * w_old + u_new * w_new
                if p == 1:
                    m_s[rows, :] = m_all
                    d_s[rows, :] = d_all
                    u_s[rows, :] = u_all
                else:
                    o_ref[rows, :] = (u_all / d_all).astype(o_ref.dtype)
            return carry

        lax.fori_loop(0, n_blocks, block, 0, unroll=BLOCK_UNROLL)


def _attention(qkv_nat, qkv_r4, qkv_r16, slopes):
    B, S, _ = qkv_nat.shape
    n_slabs = ATT_WIDTH // LANES

    def slab(part):
        return pl.BlockSpec((None, S, LANES), lambda b, h, slopes_ref: (b, 0, part * n_slabs + h))

    qkv_specs = [slab(0), slab(1), slab(2)]
    n_br = len(DIL_PATTERNS)
    return pl.pallas_call(
        functools.partial(_attn_body, seq=S),
        grid_spec=pltpu.PrefetchScalarGridSpec(
            num_scalar_prefetch=1,
            grid=(B, n_slabs),
            in_specs=qkv_specs * n_br,
            out_specs=pl.BlockSpec((None, S, LANES), lambda b, h, slopes_ref: (b, 0, h)),
            scratch_shapes=[
                pltpu.VMEM((n_br, N_BIAS_VARIANTS, HEADS_PER_SLAB, SPAN, 2 * SPAN), F32),
                pltpu.VMEM((S, LANES), F32),
                pltpu.VMEM((S, LANES), F32),
                pltpu.VMEM((S, LANES), F32),
            ],
        ),
        out_shape=jax.ShapeDtypeStruct((B, S, ATT_WIDTH), BF16),
        compiler_params=pltpu.CompilerParams(
            dimension_semantics=("arbitrary", "arbitrary"),
            vmem_limit_bytes=VMEM_LIMIT_BYTES),
        name="dilated_attention",
    )(slopes, qkv_nat, qkv_nat, qkv_nat, qkv_r4, qkv_r4, qkv_r4, qkv_r16, qkv_r16, qkv_r16)


FF_CHUNK = 256


def _ffn_body(x_ref, att_ref, pool_ref, p_ref, wout_ref, lnf_ref, wup_ref, cw_ref, cb_ref, wdn_ref,
              lnp_ref, wg_ref, wple_ref, lnfin_ref, o_ref, act_s, carry_s, *, tm):
    s_idx = pl.program_id(1)
    h = x_ref[...]
    h = h + jnp.dot(att_ref[...], wout_ref[0:ATT_WIDTH, :], preferred_element_type=F32)
    h = h + jnp.dot(pool_ref[...], wout_ref[ATT_WIDTH:, :], preferred_element_type=F32)

    hn = _rms_scale(h, lnf_ref[...]).astype(BF16)

    @pl.when(s_idx == 0)
    def _():
        carry_s[...] = jnp.zeros_like(carry_s)

    def conv(c0):
        cols = slice(c0, c0 + FF_CHUNK)
        pre = jnp.dot(hn, wup_ref[:, cols], preferred_element_type=F32)
        prev = carry_s[:, cols]
        ext = jnp.concatenate([prev, pre], axis=0)
        sh1 = pltpu.roll(ext, 1, 0)[SUBLANES:]
        sh2 = pltpu.roll(ext, 2, 0)[SUBLANES:]
        carry_s[:, cols] = pre[tm - SUBLANES:, :]
        y = cb_ref[:, cols] + cw_ref[0:1, cols] * sh2
        y = y + cw_ref[1:2, cols] * sh1
        return y + cw_ref[2:3, cols] * pre

    for c in range(D_FF // FF_CHUNK):
        c0 = c * FF_CHUNK
        gate = conv(c0)
        val = conv(D_FF + c0)
        act_s[:, c0:c0 + FF_CHUNK] = (gate * jax.nn.sigmoid(gate) * val).astype(BF16)

    h = h + jnp.dot(act_s[...], wdn_ref[...], preferred_element_type=F32)

    hn = _rms_scale(h, lnp_ref[...]).astype(BF16)
    g = jax.nn.sigmoid(jnp.dot(hn, wg_ref[...], preferred_element_type=F32))
    emb = jnp.dot(p_ref[...].astype(BF16), wple_ref[...], preferred_element_type=F32)
    h = h + g * emb
    o_ref[...] = _rms_scale(h, lnfin_ref[...])


def _ffn(x, att, pool, p, w_out, ln_ffn, w_up, conv_w, conv_b, w_down, ln_ple, w_ple_gate, w_ple,
         ln_final, *, tm):
    B, S, D = x.shape
    grid = (B, S // tm)
    row = lambda b, s: (b, s, 0)
    const2 = lambda b, s: (0, 0)

    def resident(shape):
        return pl.BlockSpec(shape, const2, pipeline_mode=pl.Buffered(1))

    return pl.pallas_call(
        functools.partial(_ffn_body, tm=tm),
        grid=grid,
        in_specs=[
            pl.BlockSpec((None, tm, D), row),
            pl.BlockSpec((None, tm, ATT_WIDTH), row),
            pl.BlockSpec((None, tm, POOL_WIDTH), row),
            pl.BlockSpec((None, tm, PLE_DIM), row),
            resident((ATT_WIDTH + POOL_WIDTH, D)),
            resident((1, D)),
            resident((D, 2 * D_FF)),
            resident((CONV_WIDTH, 2 * D_FF)),
            resident((1, 2 * D_FF)),
            resident((D_FF, D)),
            resident((1, D)),
            resident((D, D)),
            resident((PLE_DIM, D)),
            resident((1, D)),
        ],
        out_specs=pl.BlockSpec((None, tm, D), row),
        out_shape=jax.ShapeDtypeStruct((B, S, D), F32),
        scratch_shapes=[
            pltpu.VMEM((tm, D_FF), BF16),
            pltpu.VMEM((SUBLANES, 2 * D_FF), F32),
        ],
        compiler_params=pltpu.CompilerParams(
            dimension_semantics=("arbitrary", "arbitrary"),
            vmem_limit_bytes=VMEM_LIMIT_BYTES),
        name="outproj_ffn_ple",
    )(x, att, pool, p, w_out.astype(BF16), ln_ffn.reshape(1, D), w_up.astype(BF16), conv_w,
      conv_b.reshape(1, 2 * D_FF), w_down.astype(BF16), ln_ple.reshape(1, D), w_ple_gate.astype(BF16),
      w_ple.astype(BF16), ln_final.reshape(1, D))


def kernel(x, p, ln_mix, w_in, pool_w, pool_scale, w_out, ln_ffn, w_up, conv_w, conv_b, w_down, ln_ple,
           w_ple_gate, w_ple, ln_final):
    depth = p.shape[0]
    slopes = jnp.exp2(-8.0 * (jnp.arange(N_ATT_HEADS, dtype=F32) + 1.0) / N_ATT_HEADS)
    assert depth == 1, "the final RMSNorm is fused into the last layer's kernel"
    i = 0
    qkv_nat, qkv_r4, qkv_r16, pool = _in_proj(x, ln_mix[i], w_in[i], pool_w[i], pool_scale[i], tm=512)
    att = _attention(qkv_nat, qkv_r4, qkv_r16, slopes)
    return _ffn(x, att, pool, p[i], w_out[i], ln_ffn[i], w_up[i], conv_w[i], conv_b[i], w_down[i],
                ln_ple[i], w_ple_gate[i], w_ple[i], ln_final, tm=512)
```

```python
import functools

import jax
import jax.numpy as jnp
from jax import lax
from jax.experimental import pallas as pl
from jax.experimental.pallas import tpu as pltpu

D_MODEL = 1024
HEAD_DIM = 64
ATT_WIDTH = 512
N_ATT_HEADS = ATT_WIDTH // HEAD_DIM
DIL_PATTERNS = ((128, 1), (512, 4), (2048, 16))
SPAN = 128
POOL_WIDTH = 512
POOL_WINDOWS = (2, 4, 8, 16)
POOL_GROUP_DIM = 128
MAX_POOL_WINDOW = 16
IN_WIDTH = 3 * ATT_WIDTH + POOL_WIDTH
D_FF = 2816
CONV_WIDTH = 3
PLE_DIM = 256
EPS = 1e-6

LANES = 128
SUBLANES = 8
HEADS_PER_SLAB = LANES // HEAD_DIM
VMEM_LIMIT_BYTES = 56 * 1024 * 1024

F32 = jnp.float32
BF16 = jnp.bfloat16


def _rms_scale(t, g):
    inv = lax.rsqrt(jnp.mean(t * t, axis=-1, keepdims=True) + EPS)
    return t * inv * g


QKV_WIDTH = 3 * ATT_WIDTH
LOG2E = 1.4426950408889634
Q_SCALE = (HEAD_DIM ** -0.5) * LOG2E
N_QKV_SLABS = QKV_WIDTH // LANES


def _in_proj_body(x_ref, g_ref, w_ref, pw_ref, ps_ref, nat_ref, r4_ref, r16_ref, pool_ref, zbuf, z4buf,
                  ucarry, hn_s, *, tm):
    s_idx = pl.program_id(1)
    hw = MAX_POOL_WINDOW

    @pl.when(s_idx == 0)
    def _():
        ucarry[...] = jnp.zeros_like(ucarry)

    hn_s[...] = _rms_scale(x_ref[...], g_ref[...]).astype(BF16)
    a = ATT_WIDTH

    u = jnp.dot(hn_s[...], w_ref[:, 3 * a:], preferred_element_type=F32)
    e = jnp.concatenate([ucarry[...], u], axis=0)
    ucarry[...] = u[tm - hw:, :]
    gd = POOL_GROUP_DIM
    s2 = e + pltpu.roll(e, 1, 0)
    s4 = s2[:, gd:] + pltpu.roll(s2[:, gd:], 2, 0)
    s8 = s4[:, gd:] + pltpu.roll(s4[:, gd:], 4, 0)
    s16 = s8[:, gd:] + pltpu.roll(s8[:, gd:], 8, 0)
    sums = (s2[hw:, 0:gd], s4[hw:, 0:gd], s8[hw:, 0:gd], s16[hw:, :])
    t1 = (s_idx * tm + lax.broadcasted_iota(jnp.int32, (tm, 1), 0) + 1).astype(F32)
    dlts = []
    for g, w in enumerate(POOL_WINDOWS):
        inv_cnt = 1.0 / jnp.minimum(t1, float(w))
        dlts.append((sums[g] * inv_cnt - u[:, g * gd:(g + 1) * gd]).astype(BF16))

    slabs_per_part = a // LANES
    n4 = tm // 4
    for part in range(3):
        z = jnp.dot(hn_s[...], w_ref[:, part * a:(part + 1) * a], preferred_element_type=F32)
        if part == 0:
            z = z * Q_SCALE
        nat_ref[:, part * a:(part + 1) * a] = z.astype(BF16)
        for sl in range(slabs_per_part):
            j = part * slabs_per_part + sl
            cols = slice(j * LANES, (j + 1) * LANES)
            zbuf[j] = z[:, sl * LANES:(sl + 1) * LANES]
            for c in range(4):
                cls4 = zbuf[j, pl.ds(c, n4, stride=4), :]
                r4_ref[c, :, cols] = cls4.astype(BF16)
                z4buf[j, c * n4:(c + 1) * n4, :] = cls4
            for c in range(4):
                for b in range(4):
                    cls16 = z4buf[j, pl.ds(c * n4 + b, tm // 16, stride=4), :]
                    r16_ref[4 * b + c, :, cols] = cls16.astype(BF16)

    for g in range(len(POOL_WINDOWS)):
        y = jnp.dot(dlts[g], pw_ref[g], preferred_element_type=F32)
        pool_ref[:, g * gd:(g + 1) * gd] = (y * ps_ref[:, g * gd:(g + 1) * gd]).astype(BF16)


def _in_proj(x, ln_mix, w_in, pool_w, pool_scale, *, tm):
    B, S, D = x.shape
    grid = (B, S // tm)
    row = lambda b, s: (b, s, 0)
    cls_row = lambda b, s: (b, 0, s, 0)
    const2 = lambda b, s: (0, 0)
    const3 = lambda b, s: (0, 0, 0)
    nat, r4, r16, pool = pl.pallas_call(
        functools.partial(_in_proj_body, tm=tm),
        grid=grid,
        in_specs=[
            pl.BlockSpec((None, tm, D), row),
            pl.BlockSpec((1, D), const2),
            pl.BlockSpec((D, IN_WIDTH), const2),
            pl.BlockSpec((len(POOL_WINDOWS), POOL_GROUP_DIM, POOL_GROUP_DIM), const3),
            pl.BlockSpec((1, POOL_WIDTH), const2),
        ],
        out_specs=[
            pl.BlockSpec((None, tm, QKV_WIDTH), row),
            pl.BlockSpec((None, 4, tm // 4, QKV_WIDTH), cls_row),
            pl.BlockSpec((None, 16, tm // 16, QKV_WIDTH), cls_row),
            pl.BlockSpec((None, tm, POOL_WIDTH), row),
        ],
        out_shape=[
            jax.ShapeDtypeStruct((B, S, QKV_WIDTH), BF16),
            jax.ShapeDtypeStruct((B, 4, S // 4, QKV_WIDTH), BF16),
            jax.ShapeDtypeStruct((B, 16, S // 16, QKV_WIDTH), BF16),
            jax.ShapeDtypeStruct((B, S, POOL_WIDTH), BF16),
        ],
        scratch_shapes=[
            pltpu.VMEM((N_QKV_SLABS, tm, LANES), F32),
            pltpu.VMEM((N_QKV_SLABS, tm, LANES), F32),
            pltpu.VMEM((MAX_POOL_WINDOW, POOL_WIDTH), F32),
            pltpu.VMEM((tm, D), BF16),
        ],
        compiler_params=pltpu.CompilerParams(
            dimension_semantics=("arbitrary", "arbitrary"),
            vmem_limit_bytes=VMEM_LIMIT_BYTES),
        name="in_proj_pool",
    )(x, ln_mix.reshape(1, D), w_in.astype(BF16), pool_w.astype(BF16), pool_scale.reshape(1, POOL_WIDTH))
    return nat, r4.reshape(B, S, QKV_WIDTH), r16.reshape(B, S, QKV_WIDTH), pool


BLOCK_UNROLL = 8
N_BIAS_VARIANTS = 3


def _attn_body(slopes_ref, qn_ref, kn_ref, vn_ref, q4_ref, k4_ref, v4_ref, q16_ref, k16_ref, v16_ref,
               o_ref, bias_s, m_s, d_s, u_s, *, seq):
    hp = pl.program_id(1)
    n_blocks = seq // SPAN
    lane = lax.broadcasted_iota(jnp.int32, (1, LANES), 1)
    is_h0 = lane < HEAD_DIM
    head_mask = (is_h0, jnp.logical_not(is_h0))
    head_sel = tuple(jnp.where(hm, 1.0, 0.0).astype(BF16) for hm in head_mask)
    ones_cols = tuple(jnp.broadcast_to(sel, (2 * SPAN, LANES)) for sel in head_sel)

    qi = lax.broadcasted_iota(jnp.int32, (SPAN, 2 * SPAN), 0)
    kj = lax.broadcasted_iota(jnp.int32, (SPAN, 2 * SPAN), 1)
    diffs = (qi - kj, qi + SPAN - kj, qi + SPAN - kj)
    extra = (None, kj >= SPAN, None)
    for p, (_, dil) in enumerate(DIL_PATTERNS):
        for hh in range(HEADS_PER_SLAB):
            slope = slopes_ref[hp * HEADS_PER_SLAB + hh]
            for var in range(N_BIAS_VARIANTS):
                diff = diffs[var]
                valid = (diff >= 0) & (diff <= SPAN)
                if extra[var] is not None:
                    valid = valid & extra[var]
                penalty = (-(slope * (float(dil) * LOG2E))) * diff.astype(F32)
                bias_s[p, var, hh] = jnp.where(valid, penalty, -jnp.inf)

    branch_refs = ((qn_ref, kn_ref, vn_ref), (q4_ref, k4_ref, v4_ref), (q16_ref, k16_ref, v16_ref))
    for p in (2, 1, 0):
        dil = DIL_PATTERNS[p][1]
        blocks_per_class = n_blocks // dil
        q_ref, k_ref, v_ref = branch_refs[p]

        def block(g, carry, p=p, dil=dil, blocks_per_class=blocks_per_class,
                  q_ref=q_ref, k_ref=k_ref, v_ref=v_ref):
            cls = g // blocks_per_class
            jb = g % blocks_per_class
            variant = jnp.where(g == 0, 0, jnp.where(jb == 0, 1, 2))
            row0 = pl.multiple_of(g * SPAN, SPAN)
            win0 = pl.multiple_of(jnp.maximum(g - 1, 0) * SPAN, SPAN)
            qb = q_ref[pl.ds(row0, SPAN), :]
            kw = k_ref[pl.ds(win0, 2 * SPAN), :]
            vw = v_ref[pl.ds(win0, 2 * SPAN), :]
            ms, es, vs = [], [], []
            for hh in range(HEADS_PER_SLAB):
                qh = qb * head_sel[hh]
                s = lax.dot_general(qh, kw, (((1,), (1,)), ((), ())), preferred_element_type=F32)
                s = s + bias_s[p, variant, hh]
                m = jnp.max(s, axis=1, keepdims=True)
                ms.append(m)
                es.append(jnp.exp2(s - m).astype(BF16))
                vs.append(jnp.concatenate(
                    [jnp.where(head_mask[hh], vw, jnp.zeros_like(vw)), ones_cols[hh]], axis=1))
            ud = jnp.dot(jnp.concatenate(es, axis=1), jnp.concatenate(vs, axis=0),
                         preferred_element_type=F32)
            m_new = jnp.where(is_h0, ms[0], ms[1])
            u_new = ud[:, 0:LANES]
            d_new = ud[:, LANES:]
            start = (dil * SPAN) * jb + cls
            if dil == 1:
                rows = pl.ds(pl.multiple_of(start, SPAN), SPAN)
            else:
                rows = pl.ds(start, SPAN, stride=dil)
            if p == 2:
                m_s[rows, :] = m_new
                d_s[rows, :] = d_new
                u_s[rows, :] = u_new
            else:
                m_old = m_s[rows, :]
                m_all = jnp.maximum(m_old, m_new)
                w_old = jnp.exp2(m_old - m_all)
                w_new = jnp.exp2(m_new - m_all)
                d_all = d_s[rows, :] * w_old + d_new * w_new
                u_all = u_s[rows, :] * w_old + u_new * w_new
                if p == 1:
                    m_s[rows, :] = m_all
                    d_s[rows, :] = d_all
                    u_s[rows, :] = u_all
                else:
                    o_ref[rows, :] = (u_all / d_all).astype(o_ref.dtype)
            return carry

        lax.fori_loop(0, n_blocks, block, 0, unroll=BLOCK_UNROLL)


def _attention(qkv_nat, qkv_r4, qkv_r16, slopes):
    B, S, _ = qkv_nat.shape
    n_slabs = ATT_WIDTH // LANES

    def slab(part):
        return pl.BlockSpec((None, S, LANES), lambda b, h, slopes_ref: (b, 0, part * n_slabs + h))

    qkv_specs = [slab(0), slab(1), slab(2)]
    n_br = len(DIL_PATTERNS)
    return pl.pallas_call(
        functools.partial(_attn_body, seq=S),
        grid_spec=pltpu.PrefetchScalarGridSpec(
            num_scalar_prefetch=1,
            grid=(B, n_slabs),
            in_specs=qkv_specs * n_br,
            out_specs=pl.BlockSpec((None, S, LANES), lambda b, h, slopes_ref: (b, 0, h)),
            scratch_shapes=[
                pltpu.VMEM((n_br, N_BIAS_VARIANTS, HEADS_PER_SLAB, SPAN, 2 * SPAN), F32),
                pltpu.VMEM((S, LANES), F32),
                pltpu.VMEM((S, LANES), F32),
                pltpu.VMEM((S, LANES), F32),
            ],
        ),
        out_shape=jax.ShapeDtypeStruct((B, S, ATT_WIDTH), BF16),
        compiler_params=pltpu.CompilerParams(
            dimension_semantics=("arbitrary", "arbitrary"),
            vmem_limit_bytes=VMEM_LIMIT_BYTES),
        name="dilated_attention",
    )(slopes, qkv_nat, qkv_nat, qkv_nat, qkv_r4, qkv_r4, qkv_r4, qkv_r16, qkv_r16, qkv_r16)


FF_CHUNK = 256


def _ffn_body(x_ref, att_ref, pool_ref, p_ref, wout_ref, lnf_ref, wup_ref, cw_ref, cb_ref, wdn_ref,
              lnp_ref, wg_ref, wple_ref, lnfin_ref, o_ref, act_s, carry_s, *, tm):
    s_idx = pl.program_id(1)

    @pl.when(s_idx == 0)
    def _():
        carry_s[...] = jnp.zeros_like(carry_s)

    h = x_ref[...]
    h = h + jnp.dot(att_ref[...], wout_ref[0:ATT_WIDTH, :], preferred_element_type=F32)
    h = h + jnp.dot(pool_ref[...], wout_ref[ATT_WIDTH:, :], preferred_element_type=F32)

    hn = _rms_scale(h, lnf_ref[...]).astype(BF16)

    def conv(c0):
        cols = slice(c0, c0 + FF_CHUNK)
        pre = jnp.dot(hn, wup_ref[:, cols], preferred_element_type=F32)
        prev = carry_s[:, cols]
        ext = jnp.concatenate([prev, pre], axis=0)
        sh1 = pltpu.roll(ext, 1, 0)[SUBLANES:]
        sh2 = pltpu.roll(ext, 2, 0)[SUBLANES:]
        carry_s[:, cols] = pre[tm - SUBLANES:, :]
        y = cb_ref[:, cols] + cw_ref[0:1, cols] * sh2
        y = y + cw_ref[1:2, cols] * sh1
        return y + cw_ref[2:3, cols] * pre

    for c in range(D_FF // FF_CHUNK):
        c0 = c * FF_CHUNK
        gate = conv(c0)
        val = conv(D_FF + c0)
        act_s[:, c0:c0 + FF_CHUNK] = (gate * jax.nn.sigmoid(gate) * val).astype(BF16)

    h = h + jnp.dot(act_s[...], wdn_ref[...], preferred_element_type=F32)

    hn = _rms_scale(h, lnp_ref[...]).astype(BF16)
    g = jax.nn.sigmoid(jnp.dot(hn, wg_ref[...], preferred_element_type=F32))
    emb = jnp.dot(p_ref[...].astype(BF16), wple_ref[...], preferred_element_type=F32)
    h = h + g * emb
    o_ref[...] = _rms_scale(h, lnfin_ref[...])


def _ffn(x, att, pool, p, w_out, ln_ffn, w_up, conv_w, conv_b, w_down, ln_ple, w_ple_gate, w_ple,
         ln_final, *, tm):
    B, S, D = x.shape
    grid = (B, S // tm)
    row = lambda b, s: (b, s, 0)
    const2 = lambda b, s: (0, 0)

    def resident(shape):
        return pl.BlockSpec(shape, const2, pipeline_mode=pl.Buffered(1))

    return pl.pallas_call(
        functools.partial(_ffn_body, tm=tm),
        grid=grid,
        in_specs=[
            pl.BlockSpec((None, tm, D), row),
            pl.BlockSpec((None, tm, ATT_WIDTH), row),
            pl.BlockSpec((None, tm, POOL_WIDTH), row),
            pl.BlockSpec((None, tm, PLE_DIM), row),
            resident((ATT_WIDTH + POOL_WIDTH, D)),
            resident((1, D)),
            resident((D, 2 * D_FF)),
            resident((CONV_WIDTH, 2 * D_FF)),
            resident((1, 2 * D_FF)),
            resident((D_FF, D)),
            resident((1, D)),
            resident((D, D)),
            resident((PLE_DIM, D)),
            resident((1, D)),
        ],
        out_specs=pl.BlockSpec((None, tm, D), row),
        out_shape=jax.ShapeDtypeStruct((B, S, D), F32),
        scratch_shapes=[
            pltpu.VMEM((tm, D_FF), BF16),
            pltpu.VMEM((SUBLANES, 2 * D_FF), F32),
        ],
        compiler_params=pltpu.CompilerParams(
            dimension_semantics=("arbitrary", "arbitrary"),
            vmem_limit_bytes=VMEM_LIMIT_BYTES),
        name="outproj_ffn_ple",
    )(x, att, pool, p, w_out.astype(BF16), ln_ffn.reshape(1, D), w_up.astype(BF16), conv_w,
      conv_b.reshape(1, 2 * D_FF), w_down.astype(BF16), ln_ple.reshape(1, D), w_ple_gate.astype(BF16),
      w_ple.astype(BF16), ln_final.reshape(1, D))


def kernel(x, p, ln_mix, w_in, pool_w, pool_scale, w_out, ln_ffn, w_up, conv_w, conv_b, w_down, ln_ple,
           w_ple_gate, w_ple, ln_final):
    depth = p.shape[0]
    slopes = jnp.exp2(-8.0 * (jnp.arange(N_ATT_HEADS, dtype=F32) + 1.0) / N_ATT_HEADS)
    assert depth == 1, "the final RMSNorm is fused into the last layer's kernel"
    i = 0
    qkv_nat, qkv_r4, qkv_r16, pool = _in_proj(x, ln_mix[i], w_in[i], pool_w[i], pool_scale[i], tm=512)
    att = _attention(qkv_nat, qkv_r4, qkv_r16, slopes)
    return _ffn(x, att, pool, p[i], w_out[i], ln_ffn[i], w_up[i], conv_w[i], conv_b[i], w_down[i],
                ln_ple[i], w_ple_gate[i], w_ple[i], ln_final, tm=512)
```

```python
import functools

import jax
import jax.numpy as jnp
from jax import lax
from jax.experimental import pallas as pl
from jax.experimental.pallas import tpu as pltpu

D_MODEL = 1024
HEAD_DIM = 64
ATT_WIDTH = 512
N_ATT_HEADS = ATT_WIDTH // HEAD_DIM
DIL_PATTERNS = ((128, 1), (512, 4), (2048, 16))
SPAN = 128
POOL_WIDTH = 512
POOL_WINDOWS = (2, 4, 8, 16)
POOL_GROUP_DIM = 128
MAX_POOL_WINDOW = 16
IN_WIDTH = 3 * ATT_WIDTH + POOL_WIDTH
D_FF = 2816
CONV_WIDTH = 3
PLE_DIM = 256
EPS = 1e-6

LANES = 128
SUBLANES = 8
HEADS_PER_SLAB = LANES // HEAD_DIM
VMEM_LIMIT_BYTES = 56 * 1024 * 1024

F32 = jnp.float32
BF16 = jnp.bfloat16
U32 = jnp.uint32


def _pack_rows(t):
    return pltpu.bitcast(t.astype(BF16), U32)


def _unpack_rows(t):
    return pltpu.bitcast(t, BF16)


def _rms_scale(t, g):
    inv = lax.rsqrt(jnp.mean(t * t, axis=-1, keepdims=True) + EPS)
    return t * inv * g


QKV_WIDTH = 3 * ATT_WIDTH
LOG2E = 1.4426950408889634
Q_SCALE = (HEAD_DIM ** -0.5) * LOG2E
N_QKV_SLABS = QKV_WIDTH // LANES


def _in_proj_body(x_ref, g_ref, w_ref, pw_ref, ps_ref, nat_ref, r4_ref, r16_ref, pool_ref, zbuf, z4buf,
                  ucarry, *, tm):
    s_idx = pl.program_id(1)
    hw = MAX_POOL_WINDOW

    @pl.when(s_idx == 0)
    def _():
        ucarry[...] = jnp.zeros_like(ucarry)

    hn = _rms_scale(x_ref[...], g_ref[...]).astype(BF16)
    a = ATT_WIDTH

    u = jnp.dot(hn, w_ref[:, 3 * a:], preferred_element_type=F32)
    e = jnp.concatenate([ucarry[...], u], axis=0)
    ucarry[...] = u[tm - hw:, :]
    gd = POOL_GROUP_DIM
    s2 = e + pltpu.roll(e, 1, 0)
    s4 = s2[:, gd:] + pltpu.roll(s2[:, gd:], 2, 0)
    s8 = s4[:, gd:] + pltpu.roll(s4[:, gd:], 4, 0)
    s16 = s8[:, gd:] + pltpu.roll(s8[:, gd:], 8, 0)
    sums = (s2[hw:, 0:gd], s4[hw:, 0:gd], s8[hw:, 0:gd], s16[hw:, :])
    t1 = (s_idx * tm + lax.broadcasted_iota(jnp.int32, (tm, 1), 0) + 1).astype(F32)
    dlts = []
    for g, w in enumerate(POOL_WINDOWS):
        inv_cnt = 1.0 / jnp.minimum(t1, float(w))
        dlts.append((sums[g] * inv_cnt - u[:, g * gd:(g + 1) * gd]).astype(BF16))

    slabs_per_part = a // LANES
    n4 = tm // 4
    for part in range(3):
        z = jnp.dot(hn, w_ref[:, part * a:(part + 1) * a], preferred_element_type=F32)
        if part == 0:
            z = z * Q_SCALE
        nat_ref[:, part * a:(part + 1) * a] = _pack_rows(z)
        for sl in range(slabs_per_part):
            j = part * slabs_per_part + sl
            cols = slice(j * LANES, (j + 1) * LANES)
            zbuf[j] = z[:, sl * LANES:(sl + 1) * LANES]
            for c in range(4):
                cls4 = zbuf[j, pl.ds(c, n4, stride=4), :]
                r4_ref[c, :, cols] = _pack_rows(cls4)
                z4buf[j, c * n4:(c + 1) * n4, :] = cls4
            for c in range(4):
                for b in range(4):
                    cls16 = z4buf[j, pl.ds(c * n4 + b, tm // 16, stride=4), :]
                    r16_ref[4 * b + c, :, cols] = _pack_rows(cls16)

    for g in range(len(POOL_WINDOWS)):
        y = jnp.dot(dlts[g], pw_ref[g], preferred_element_type=F32)
        pool_ref[:, g * gd:(g + 1) * gd] = _pack_rows(y * ps_ref[:, g * gd:(g + 1) * gd])


def _in_proj(x, ln_mix, w_in, pool_w, pool_scale, *, tm):
    B, S, D = x.shape
    grid = (B, S // tm)
    row = lambda b, s: (b, s, 0)
    cls_row = lambda b, s: (b, 0, s, 0)
    const2 = lambda b, s: (0, 0)
    const3 = lambda b, s: (0, 0, 0)
    nat, r4, r16, pool = pl.pallas_call(
        functools.partial(_in_proj_body, tm=tm),
        grid=grid,
        in_specs=[
            pl.BlockSpec((None, tm, D), row),
            pl.BlockSpec((1, D), const2),
            pl.BlockSpec((D, IN_WIDTH), const2),
            pl.BlockSpec((len(POOL_WINDOWS), POOL_GROUP_DIM, POOL_GROUP_DIM), const3),
            pl.BlockSpec((1, POOL_WIDTH), const2),
        ],
        out_specs=[
            pl.BlockSpec((None, tm // 2, QKV_WIDTH), row),
            pl.BlockSpec((None, 4, tm // 8, QKV_WIDTH), cls_row),
            pl.BlockSpec((None, 16, tm // 32, QKV_WIDTH), cls_row),
            pl.BlockSpec((None, tm // 2, POOL_WIDTH), row),
        ],
        out_shape=[
            jax.ShapeDtypeStruct((B, S // 2, QKV_WIDTH), U32),
            jax.ShapeDtypeStruct((B, 4, S // 8, QKV_WIDTH), U32),
            jax.ShapeDtypeStruct((B, 16, S // 32, QKV_WIDTH), U32),
            jax.ShapeDtypeStruct((B, S // 2, POOL_WIDTH), U32),
        ],
        scratch_shapes=[
            pltpu.VMEM((N_QKV_SLABS, tm, LANES), F32),
            pltpu.VMEM((N_QKV_SLABS, tm, LANES), F32),
            pltpu.VMEM((MAX_POOL_WINDOW, POOL_WIDTH), F32),
        ],
        compiler_params=pltpu.CompilerParams(
            dimension_semantics=("arbitrary", "arbitrary"),
            vmem_limit_bytes=VMEM_LIMIT_BYTES),
        name="in_proj_pool",
    )(x, ln_mix.reshape(1, D), w_in.astype(BF16), pool_w.astype(BF16), pool_scale.reshape(1, POOL_WIDTH))
    return nat, r4.reshape(B, S // 2, QKV_WIDTH), r16.reshape(B, S // 2, QKV_WIDTH), pool


BLOCK_UNROLL = 16
N_BIAS_VARIANTS = 3


def _attn_body(slopes_ref, qn_ref, kn_ref, vn_ref, q4_ref, k4_ref, v4_ref, q16_ref, k16_ref, v16_ref,
               o_ref, bias_s, m_s, d_s, u_s, *, seq):
    hp = pl.program_id(1)
    n_blocks = seq // SPAN
    half = SPAN // 2
    lane = lax.broadcasted_iota(jnp.int32, (1, LANES), 1)
    is_h0 = lane < HEAD_DIM
    head_mask = (is_h0, jnp.logical_not(is_h0))
    head_sel = tuple(jnp.where(hm, 1.0, 0.0).astype(BF16) for hm in head_mask)
    ones_cols = tuple(jnp.broadcast_to(sel, (2 * SPAN, LANES)) for sel in head_sel)

    qi = lax.broadcasted_iota(jnp.int32, (SPAN, 2 * SPAN), 0)
    kj = lax.broadcasted_iota(jnp.int32, (SPAN, 2 * SPAN), 1)
    diffs = (qi - kj, qi + SPAN - kj, qi + SPAN - kj)
    extra = (None, kj >= SPAN, None)
    for p, (_, dil) in enumerate(DIL_PATTERNS):
        for hh in range(HEADS_PER_SLAB):
            slope = slopes_ref[hp * HEADS_PER_SLAB + hh]
            for var in range(N_BIAS_VARIANTS):
                diff = diffs[var]
                valid = (diff >= 0) & (diff <= SPAN)
                if extra[var] is not None:
                    valid = valid & extra[var]
                penalty = (-(slope * (float(dil) * LOG2E))) * diff.astype(F32)
                bias_s[p, var, hh] = jnp.where(valid, penalty, -jnp.inf)

    branch_refs = ((qn_ref, kn_ref, vn_ref), (q4_ref, k4_ref, v4_ref), (q16_ref, k16_ref, v16_ref))
    for p in (2, 1, 0):
        dil = DIL_PATTERNS[p][1]
        blocks_per_class = n_blocks // dil
        q_ref, k_ref, v_ref = branch_refs[p]

        def block(g, carry, p=p, dil=dil, blocks_per_class=blocks_per_class,
                  q_ref=q_ref, k_ref=k_ref, v_ref=v_ref):
            cls = g // blocks_per_class
            jb = g % blocks_per_class
            variant = jnp.where(g == 0, 0, jnp.where(jb == 0, 1, 2))
            row0 = pl.multiple_of(g * half, half)
            win0 = pl.multiple_of(jnp.maximum(g - 1, 0) * half, half)
            qb = _unpack_rows(q_ref[pl.ds(row0, half), :])
            kw = _unpack_rows(k_ref[pl.ds(win0, 2 * half), :])
            vw = _unpack_rows(v_ref[pl.ds(win0, 2 * half), :])
            ms, es, vs = [], [], []
            for hh in range(HEADS_PER_SLAB):
                qh = qb * head_sel[hh]
                s = lax.dot_general(qh, kw, (((1,), (1,)), ((), ())), preferred_element_type=F32)
                s = s + bias_s[p, variant, hh]
                m = jnp.max(s, axis=1, keepdims=True)
                ms.append(m)
                es.append(jnp.exp2(s - m).astype(BF16))
                vs.append(jnp.concatenate(
                    [jnp.where(head_mask[hh], vw, jnp.zeros_like(vw)), ones_cols[hh]], axis=1))
            ud = jnp.dot(jnp.concatenate(es, axis=1), jnp.concatenate(vs, axis=0),
                         preferred_element_type=F32)
            m_new = jnp.where(is_h0, ms[0], ms[1])
            u_new = ud[:, 0:LANES]
            d_new = ud[:, LANES:]
            start = (dil * SPAN) * jb + cls
            if dil == 1:
                rows = pl.ds(pl.multiple_of(start, SPAN), SPAN)
            else:
                rows = pl.ds(start, SPAN, stride=dil)
            if p == 2:
                m_s[rows, :] = m_new
                d_s[rows, :] = d_new
                u_s[rows, :] = u_new
            else:
                m_old = m_s[rows, :]
                m_all = jnp.maximum(m_old, m_new)
                w_old = jnp.exp2(m_old - m_all)
                w_new = jnp.exp2(m_new - m_all)
                d_all = d_s[rows, :] * w_old + d_new * w_new
                u_all = u_s[rows, :] * w_old + u_new * w_new
                if p == 1:
                    m_s[rows, :] = m_all
                    d_s[rows, :] = d_all
                    u_s[rows, :] = u_all
                else:
                    o_ref[pl.ds(row0, half), :] = _pack_rows(u_all / d_all)
            return carry

        lax.fori_loop(0, n_blocks, block, 0, unroll=BLOCK_UNROLL)


def _attention(qkv_nat, qkv_r4, qkv_r16, slopes):
    B, half_s, _ = qkv_nat.shape
    S = 2 * half_s
    n_slabs = ATT_WIDTH // LANES

    def slab(part):
        return pl.BlockSpec((None, half_s, LANES), lambda b, h, slopes_ref: (b, 0, part * n_slabs + h))

    qkv_specs = [slab(0), slab(1), slab(2)]
    n_br = len(DIL_PATTERNS)
    return pl.pallas_call(
        functools.partial(_attn_body, seq=S),
        grid_spec=pltpu.PrefetchScalarGridSpec(
            num_scalar_prefetch=1,
            grid=(B, n_slabs),
            in_specs=qkv_specs * n_br,
            out_specs=pl.BlockSpec((None, half_s, LANES), lambda b, h, slopes_ref: (b, 0, h)),
            scratch_shapes=[
                pltpu.VMEM((n_br, N_BIAS_VARIANTS, HEADS_PER_SLAB, SPAN, 2 * SPAN), F32),
                pltpu.VMEM((S, LANES), F32),
                pltpu.VMEM((S, LANES), F32),
                pltpu.VMEM((S, LANES), F32),
            ],
        ),
        out_shape=jax.ShapeDtypeStruct((B, half_s, ATT_WIDTH), U32),
        compiler_params=pltpu.CompilerParams(
            dimension_semantics=("arbitrary", "arbitrary"),
            vmem_limit_bytes=VMEM_LIMIT_BYTES),
        name="dilated_attention",
    )(slopes, qkv_nat, qkv_nat, qkv_nat, qkv_r4, qkv_r4, qkv_r4, qkv_r16, qkv_r16, qkv_r16)


FF_CHUNK = 256


def _ffn_body(x_ref, att_ref, pool_ref, p_ref, wout_ref, lnf_ref, wup_ref, cw_ref, cb_ref, wdn_ref,
              lnp_ref, wg_ref, wple_ref, lnfin_ref, o_ref, act_s, carry_s, *, tm):
    s_idx = pl.program_id(1)

    @pl.when(s_idx == 0)
    def _():
        carry_s[...] = jnp.zeros_like(carry_s)

    h = x_ref[...]
    h = h + jnp.dot(_unpack_rows(att_ref[...]), wout_ref[0:ATT_WIDTH, :], preferred_element_type=F32)
    h = h + jnp.dot(_unpack_rows(pool_ref[...]), wout_ref[ATT_WIDTH:, :], preferred_element_type=F32)

    hn = _rms_scale(h, lnf_ref[...]).astype(BF16)

    def conv(c0):
        cols = slice(c0, c0 + FF_CHUNK)
        pre = jnp.dot(hn, wup_ref[:, cols], preferred_element_type=F32)
        prev = carry_s[:, cols]
        ext = jnp.concatenate([prev, pre], axis=0)
        sh1 = pltpu.roll(ext, 1, 0)[SUBLANES:]
        sh2 = pltpu.roll(ext, 2, 0)[SUBLANES:]
        carry_s[:, cols] = pre[tm - SUBLANES:, :]
        y = cb_ref[:, cols] + cw_ref[0:1, cols] * sh2
        y = y + cw_ref[1:2, cols] * sh1
        return y + cw_ref[2:3, cols] * pre

    for c in range(D_FF // FF_CHUNK):
        c0 = c * FF_CHUNK
        gate = conv(c0)
        val = conv(D_FF + c0)
        act_s[:, c0:c0 + FF_CHUNK] = (gate * jax.nn.sigmoid(gate) * val).astype(BF16)

    h = h + jnp.dot(act_s[...], wdn_ref[...], preferred_element_type=F32)

    hn = _rms_scale(h, lnp_ref[...]).astype(BF16)
    g = jax.nn.sigmoid(jnp.dot(hn, wg_ref[...], preferred_element_type=F32))
    emb = jnp.dot(p_ref[...].astype(BF16), wple_ref[...], preferred_element_type=F32)
    h = h + g * emb
    o_ref[...] = _rms_scale(h, lnfin_ref[...])


def _ffn(x, att, pool, p, w_out, ln_ffn, w_up, conv_w, conv_b, w_down, ln_ple, w_ple_gate, w_ple,
         ln_final, *, tm):
    B, S, D = x.shape
    grid = (B, S // tm)
    row = lambda b, s: (b, s, 0)
    const2 = lambda b, s: (0, 0)

    def resident(shape):
        return pl.BlockSpec(shape, const2, pipeline_mode=pl.Buffered(1))

    return pl.pallas_call(
        functools.partial(_ffn_body, tm=tm),
        grid=grid,
        in_specs=[
            pl.BlockSpec((None, tm, D), row),
            pl.BlockSpec((None, tm // 2, ATT_WIDTH), row),
            pl.BlockSpec((None, tm // 2, POOL_WIDTH), row),
            pl.BlockSpec((None, tm, PLE_DIM), row),
            resident((ATT_WIDTH + POOL_WIDTH, D)),
            resident((1, D)),
            resident((D, 2 * D_FF)),
            resident((CONV_WIDTH, 2 * D_FF)),
            resident((1, 2 * D_FF)),
            resident((D_FF, D)),
            resident((1, D)),
            resident((D, D)),
            resident((PLE_DIM, D)),
            resident((1, D)),
        ],
        out_specs=pl.BlockSpec((None, tm, D), row),
        out_shape=jax.ShapeDtypeStruct((B, S, D), F32),
        scratch_shapes=[
            pltpu.VMEM((tm, D_FF), BF16),
            pltpu.VMEM((SUBLANES, 2 * D_FF), F32),
        ],
        compiler_params=pltpu.CompilerParams(
            dimension_semantics=("arbitrary", "arbitrary"),
            vmem_limit_bytes=VMEM_LIMIT_BYTES),
        name="outproj_ffn_ple",
    )(x, att, pool, p, w_out.astype(BF16), ln_ffn.reshape(1, D), w_up.astype(BF16), conv_w,
      conv_b.reshape(1, 2 * D_FF), w_down.astype(BF16), ln_ple.reshape(1, D), w_ple_gate.astype(BF16),
      w_ple.astype(BF16), ln_final.reshape(1, D))


def kernel(x, p, ln_mix, w_in, pool_w, pool_scale, w_out, ln_ffn, w_up, conv_w, conv_b, w_down, ln_ple,
           w_ple_gate, w_ple, ln_final):
    depth = p.shape[0]
    slopes = jnp.exp2(-8.0 * (jnp.arange(N_ATT_HEADS, dtype=F32) + 1.0) / N_ATT_HEADS)
    assert depth == 1, "the final RMSNorm is fused into the last layer's kernel"
    i = 0
    qkv_nat, qkv_r4, qkv_r16, pool = _in_proj(x, ln_mix[i], w_in[i], pool_w[i], pool_scale[i], tm=512)
    att = _attention(qkv_nat, qkv_r4, qkv_r16, slopes)
    return _ffn(x, att, pool, p[i], w_out[i], ln_ffn[i], w_up[i], conv_w[i], conv_b[i], w_down[i],
                ln_ple[i], w_ple_gate[i], w_ple[i], ln_final, tm=512)
```

```python
import functools

import jax
import jax.numpy as jnp
from jax import lax
from jax.experimental import pallas as pl
from jax.experimental.pallas import tpu as pltpu

D_MODEL = 1024
HEAD_DIM = 64
ATT_WIDTH = 512
N_ATT_HEADS = ATT_WIDTH // HEAD_DIM
DIL_PATTERNS = ((128, 1), (512, 4), (2048, 16))
SPAN = 128
POOL_WIDTH = 512
POOL_WINDOWS = (2, 4, 8, 16)
POOL_GROUP_DIM = 128
MAX_POOL_WINDOW = 16
IN_WIDTH = 3 * ATT_WIDTH + POOL_WIDTH
D_FF = 2816
CONV_WIDTH = 3
PLE_DIM = 256
EPS = 1e-6

LANES = 128
SUBLANES = 8
HEADS_PER_SLAB = LANES // HEAD_DIM
VMEM_LIMIT_BYTES = 56 * 1024 * 1024

F32 = jnp.float32
BF16 = jnp.bfloat16
U32 = jnp.uint32


def _pack_rows(t):
    return pltpu.bitcast(t.astype(BF16), U32)


def _unpack_rows(t):
    return pltpu.bitcast(t, BF16)


def _rms_scale(t, g):
    inv = lax.rsqrt(jnp.mean(t * t, axis=-1, keepdims=True) + EPS)
    return t * inv * g


QKV_WIDTH = 3 * ATT_WIDTH
LOG2E = 1.4426950408889634
Q_SCALE = (HEAD_DIM ** -0.5) * LOG2E
N_QKV_SLABS = QKV_WIDTH // LANES


def _in_proj_body(x_ref, g_ref, w_ref, pw_ref, ps_ref, nat_ref, r4_ref, r16_ref, pool_ref, zbuf, z4buf,
                  ucarry, *, tm):
    s_idx = pl.program_id(1)
    hw = MAX_POOL_WINDOW

    @pl.when(s_idx == 0)
    def _():
        ucarry[...] = jnp.zeros_like(ucarry)

    hn = _rms_scale(x_ref[...], g_ref[...]).astype(BF16)
    a = ATT_WIDTH

    u = jnp.dot(hn, w_ref[:, 3 * a:], preferred_element_type=F32)
    e = jnp.concatenate([ucarry[...], u], axis=0)
    ucarry[...] = u[tm - hw:, :]
    gd = POOL_GROUP_DIM
    s2 = e + pltpu.roll(e, 1, 0)
    s4 = s2[:, gd:] + pltpu.roll(s2[:, gd:], 2, 0)
    s8 = s4[:, gd:] + pltpu.roll(s4[:, gd:], 4, 0)
    s16 = s8[:, gd:] + pltpu.roll(s8[:, gd:], 8, 0)
    sums = (s2[hw:, 0:gd], s4[hw:, 0:gd], s8[hw:, 0:gd], s16[hw:, :])
    t1 = (s_idx * tm + lax.broadcasted_iota(jnp.int32, (tm, 1), 0) + 1).astype(F32)
    dlts = []
    for g, w in enumerate(POOL_WINDOWS):
        inv_cnt = 1.0 / jnp.minimum(t1, float(w))
        dlts.append((sums[g] * inv_cnt - u[:, g * gd:(g + 1) * gd]).astype(BF16))

    slabs_per_part = a // LANES
    n4 = tm // 4
    for part in range(3):
        z = jnp.dot(hn, w_ref[:, part * a:(part + 1) * a], preferred_element_type=F32)
        if part == 0:
            z = z * Q_SCALE
        nat_ref[:, part * a:(part + 1) * a] = _pack_rows(z)
        for sl in range(slabs_per_part):
            j = part * slabs_per_part + sl
            cols = slice(j * LANES, (j + 1) * LANES)
            zbuf[j] = z[:, sl * LANES:(sl + 1) * LANES]
            for c in range(4):
                cls4 = zbuf[j, pl.ds(c, n4, stride=4), :]
                r4_ref[c, :, cols] = _pack_rows(cls4)
                z4buf[j, c * n4:(c + 1) * n4, :] = cls4
            for c in range(4):
                for b in range(4):
                    cls16 = z4buf[j, pl.ds(c * n4 + b, tm // 16, stride=4), :]
                    r16_ref[4 * b + c, :, cols] = _pack_rows(cls16)

    for g in range(len(POOL_WINDOWS)):
        y = jnp.dot(dlts[g], pw_ref[g], preferred_element_type=F32)
        pool_ref[:, g * gd:(g + 1) * gd] = _pack_rows(y * ps_ref[:, g * gd:(g + 1) * gd])


def _in_proj(x, ln_mix, w_in, pool_w, pool_scale, *, tm):
    B, S, D = x.shape
    grid = (B, S // tm)
    row = lambda b, s: (b, s, 0)
    cls_row = lambda b, s: (b, 0, s, 0)
    const2 = lambda b, s: (0, 0)
    const3 = lambda b, s: (0, 0, 0)
    nat, r4, r16, pool = pl.pallas_call(
        functools.partial(_in_proj_body, tm=tm),
        grid=grid,
        in_specs=[
            pl.BlockSpec((None, tm, D), row),
            pl.BlockSpec((1, D), const2),
            pl.BlockSpec((D, IN_WIDTH), const2),
            pl.BlockSpec((len(POOL_WINDOWS), POOL_GROUP_DIM, POOL_GROUP_DIM), const3),
            pl.BlockSpec((1, POOL_WIDTH), const2),
        ],
        out_specs=[
            pl.BlockSpec((None, tm // 2, QKV_WIDTH), row),
            pl.BlockSpec((None, 4, tm // 8, QKV_WIDTH), cls_row),
            pl.BlockSpec((None, 16, tm // 32, QKV_WIDTH), cls_row),
            pl.BlockSpec((None, tm // 2, POOL_WIDTH), row),
        ],
        out_shape=[
            jax.ShapeDtypeStruct((B, S // 2, QKV_WIDTH), U32),
            jax.ShapeDtypeStruct((B, 4, S // 8, QKV_WIDTH), U32),
            jax.ShapeDtypeStruct((B, 16, S // 32, QKV_WIDTH), U32),
            jax.ShapeDtypeStruct((B, S // 2, POOL_WIDTH), U32),
        ],
        scratch_shapes=[
            pltpu.VMEM((N_QKV_SLABS, tm, LANES), F32),
            pltpu.VMEM((N_QKV_SLABS, tm, LANES), F32),
            pltpu.VMEM((MAX_POOL_WINDOW, POOL_WIDTH), F32),
        ],
        compiler_params=pltpu.CompilerParams(
            dimension_semantics=("arbitrary", "arbitrary"),
            vmem_limit_bytes=VMEM_LIMIT_BYTES),
        name="in_proj_pool",
    )(x, ln_mix.reshape(1, D), w_in.astype(BF16), pool_w.astype(BF16), pool_scale.reshape(1, POOL_WIDTH))
    return nat, r4.reshape(B, S // 2, QKV_WIDTH), r16.reshape(B, S // 2, QKV_WIDTH), pool


N_BIAS_VARIANTS = 3


def _attn_body(slopes_ref, qn_ref, kn_ref, vn_ref, q4_ref, k4_ref, v4_ref, q16_ref, k16_ref, v16_ref,
               o_ref, bias_s, m_s, d_s, u_s, *, seq):
    hp = pl.program_id(1)
    n_blocks = seq // SPAN
    half = SPAN // 2
    lane = lax.broadcasted_iota(jnp.int32, (1, LANES), 1)
    is_h0 = lane < HEAD_DIM
    head_mask = (is_h0, jnp.logical_not(is_h0))
    head_sel = tuple(jnp.where(hm, 1.0, 0.0).astype(BF16) for hm in head_mask)
    ones_cols = tuple(jnp.broadcast_to(sel, (2 * SPAN, LANES)) for sel in head_sel)

    qi = lax.broadcasted_iota(jnp.int32, (SPAN, 2 * SPAN), 0)
    kj = lax.broadcasted_iota(jnp.int32, (SPAN, 2 * SPAN), 1)
    diffs = (qi - kj, qi + SPAN - kj, qi + SPAN - kj)
    extra = (None, kj >= SPAN, None)
    for p, (_, dil) in enumerate(DIL_PATTERNS):
        for hh in range(HEADS_PER_SLAB):
            slope = slopes_ref[hp * HEADS_PER_SLAB + hh]
            for var in range(N_BIAS_VARIANTS):
                diff = diffs[var]
                valid = (diff >= 0) & (diff <= SPAN)
                if extra[var] is not None:
                    valid = valid & extra[var]
                penalty = (-(slope * (float(dil) * LOG2E))) * diff.astype(F32)
                bias_s[p, var, hh] = jnp.where(valid, penalty, -jnp.inf)

    branch_refs = ((qn_ref, kn_ref, vn_ref), (q4_ref, k4_ref, v4_ref), (q16_ref, k16_ref, v16_ref))
    for p in (2, 1, 0):
        dil = DIL_PATTERNS[p][1]
        blocks_per_class = n_blocks // dil
        q_ref, k_ref, v_ref = branch_refs[p]

        for g in range(n_blocks):
            cls, jb = divmod(g, blocks_per_class)
            variant = 0 if g == 0 else (1 if jb == 0 else 2)
            row0 = g * half
            win0 = max(g - 1, 0) * half
            qb = _unpack_rows(q_ref[row0:row0 + half, :])
            kw = _unpack_rows(k_ref[win0:win0 + 2 * half, :])
            vw = _unpack_rows(v_ref[win0:win0 + 2 * half, :])
            ms, es, vs = [], [], []
            for hh in range(HEADS_PER_SLAB):
                qh = qb * head_sel[hh]
                s = lax.dot_general(qh, kw, (((1,), (1,)), ((), ())), preferred_element_type=F32)
                s = s + bias_s[p, variant, hh]
                m = jnp.max(s, axis=1, keepdims=True)
                ms.append(m)
                es.append(jnp.exp2(s - m).astype(BF16))
                vs.append(jnp.concatenate(
                    [jnp.where(head_mask[hh], vw, jnp.zeros_like(vw)), ones_cols[hh]], axis=1))
            ud = jnp.dot(jnp.concatenate(es, axis=1), jnp.concatenate(vs, axis=0),
                         preferred_element_type=F32)
            m_new = jnp.where(is_h0, ms[0], ms[1])
            u_new = ud[:, 0:LANES]
            d_new = ud[:, LANES:]
            start = (dil * SPAN) * jb + cls
            rows = pl.ds(start, SPAN) if dil == 1 else pl.ds(start, SPAN, stride=dil)
            if p == 2:
                m_s[rows, :] = m_new
                d_s[rows, :] = d_new
                u_s[rows, :] = u_new
            else:
                m_old = m_s[rows, :]
                m_all = jnp.maximum(m_old, m_new)
                w_old = jnp.exp2(m_old - m_all)
                w_new = jnp.exp2(m_new - m_all)
                d_all = d_s[rows, :] * w_old + d_new * w_new
                u_all = u_s[rows, :] * w_old + u_new * w_new
                if p == 1:
                    m_s[rows, :] = m_all
                    d_s[rows, :] = d_all
                    u_s[rows, :] = u_all
                else:
                    o_ref[row0:row0 + half, :] = _pack_rows(u_all / d_all)


def _attention(qkv_nat, qkv_r4, qkv_r16, slopes):
    B, half_s, _ = qkv_nat.shape
    S = 2 * half_s
    n_slabs = ATT_WIDTH // LANES

    def slab(part):
        return pl.BlockSpec((None, half_s, LANES), lambda b, h, slopes_ref: (b, 0, part * n_slabs + h))

    qkv_specs = [slab(0), slab(1), slab(2)]
    n_br = len(DIL_PATTERNS)
    return pl.pallas_call(
        functools.partial(_attn_body, seq=S),
        grid_spec=pltpu.PrefetchScalarGridSpec(
            num_scalar_prefetch=1,
            grid=(B, n_slabs),
            in_specs=qkv_specs * n_br,
            out_specs=pl.BlockSpec((None, half_s, LANES), lambda b, h, slopes_ref: (b, 0, h)),
            scratch_shapes=[
                pltpu.VMEM((n_br, N_BIAS_VARIANTS, HEADS_PER_SLAB, SPAN, 2 * SPAN), F32),
                pltpu.VMEM((S, LANES), F32),
                pltpu.VMEM((S, LANES), F32),
                pltpu.VMEM((S, LANES), F32),
            ],
        ),
        out_shape=jax.ShapeDtypeStruct((B, half_s, ATT_WIDTH), U32),
        compiler_params=pltpu.CompilerParams(
            dimension_semantics=("arbitrary", "arbitrary"),
            vmem_limit_bytes=VMEM_LIMIT_BYTES),
        name="dilated_attention",
    )(slopes, qkv_nat, qkv_nat, qkv_nat, qkv_r4, qkv_r4, qkv_r4, qkv_r16, qkv_r16, qkv_r16)


FF_CHUNK = 256


def _ffn_body(x_ref, att_ref, pool_ref, p_ref, wout_ref, lnf_ref, wup_ref, cw_ref, cb_ref, wdn_ref,
              lnp_ref, wg_ref, wple_ref, lnfin_ref, o_ref, act_s, carry_s, *, tm):
    s_idx = pl.program_id(1)

    @pl.when(s_idx == 0)
    def _():
        carry_s[...] = jnp.zeros_like(carry_s)

    h = x_ref[...]
    h = h + jnp.dot(_unpack_rows(att_ref[...]), wout_ref[0:ATT_WIDTH, :], preferred_element_type=F32)
    h = h + jnp.dot(_unpack_rows(pool_ref[...]), wout_ref[ATT_WIDTH:, :], preferred_element_type=F32)

    hn = _rms_scale(h, lnf_ref[...]).astype(BF16)

    def conv(c0):
        cols = slice(c0, c0 + FF_CHUNK)
        pre = jnp.dot(hn, wup_ref[:, cols], preferred_element_type=F32)
        prev = carry_s[:, cols]
        ext = jnp.concatenate([prev, pre], axis=0)
        sh1 = pltpu.roll(ext, 1, 0)[SUBLANES:]
        sh2 = pltpu.roll(ext, 2, 0)[SUBLANES:]
        carry_s[:, cols] = pre[tm - SUBLANES:, :]
        y = cb_ref[:, cols] + cw_ref[0:1, cols] * sh2
        y = y + cw_ref[1:2, cols] * sh1
        return y + cw_ref[2:3, cols] * pre

    for c in range(D_FF // FF_CHUNK):
        c0 = c * FF_CHUNK
        gate = conv(c0)
        val = conv(D_FF + c0)
        act_s[:, c0:c0 + FF_CHUNK] = (gate * jax.nn.sigmoid(gate) * val).astype(BF16)

    h = h + jnp.dot(act_s[...], wdn_ref[...], preferred_element_type=F32)

    hn = _rms_scale(h, lnp_ref[...]).astype(BF16)
    g = jax.nn.sigmoid(jnp.dot(hn, wg_ref[...], preferred_element_type=F32))
    emb = jnp.dot(p_ref[...].astype(BF16), wple_ref[...], preferred_element_type=F32)
    h = h + g * emb
    o_ref[...] = _rms_scale(h, lnfin_ref[...])


def _ffn(x, att, pool, p, w_out, ln_ffn, w_up, conv_w, conv_b, w_down, ln_ple, w_ple_gate, w_ple,
         ln_final, *, tm):
    B, S, D = x.shape
    grid = (B, S // tm)
    row = lambda b, s: (b, s, 0)
    const2 = lambda b, s: (0, 0)

    def resident(shape):
        return pl.BlockSpec(shape, const2, pipeline_mode=pl.Buffered(1))

    return pl.pallas_call(
        functools.partial(_ffn_body, tm=tm),
        grid=grid,
        in_specs=[
            pl.BlockSpec((None, tm, D), row),
            pl.BlockSpec((None, tm // 2, ATT_WIDTH), row),
            pl.BlockSpec((None, tm // 2, POOL_WIDTH), row),
            pl.BlockSpec((None, tm, PLE_DIM), row),
            resident((ATT_WIDTH + POOL_WIDTH, D)),
            resident((1, D)),
            resident((D, 2 * D_FF)),
            resident((CONV_WIDTH, 2 * D_FF)),
            resident((1, 2 * D_FF)),
            resident((D_FF, D)),
            resident((1, D)),
            resident((D, D)),
            resident((PLE_DIM, D)),
            resident((1, D)),
        ],
        out_specs=pl.BlockSpec((None, tm, D), row),
        out_shape=jax.ShapeDtypeStruct((B, S, D), F32),
        scratch_shapes=[
            pltpu.VMEM((tm, D_FF), BF16),
            pltpu.VMEM((SUBLANES, 2 * D_FF), F32),
        ],
        compiler_params=pltpu.CompilerParams(
            dimension_semantics=("arbitrary", "arbitrary"),
            vmem_limit_bytes=VMEM_LIMIT_BYTES),
        name="outproj_ffn_ple",
    )(x, att, pool, p, w_out.astype(BF16), ln_ffn.reshape(1, D), w_up.astype(BF16), conv_w,
      conv_b.reshape(1, 2 * D_FF), w_down.astype(BF16), ln_ple.reshape(1, D), w_ple_gate.astype(BF16),
      w_ple.astype(BF16), ln_final.reshape(1, D))


def kernel(x, p, ln_mix, w_in, pool_w, pool_scale, w_out, ln_ffn, w_up, conv_w, conv_b, w_down, ln_ple,
           w_ple_gate, w_ple, ln_final):
    depth = p.shape[0]
    slopes = jnp.exp2(-8.0 * (jnp.arange(N_ATT_HEADS, dtype=F32) + 1.0) / N_ATT_HEADS)
    assert depth == 1, "the final RMSNorm is fused into the last layer's kernel"
    i = 0
    qkv_nat, qkv_r4, qkv_r16, pool = _in_proj(x, ln_mix[i], w_in[i], pool_w[i], pool_scale[i], tm=512)
    att = _attention(qkv_nat, qkv_r4, qkv_r16, slopes)
    return _ffn(x, att, pool, p[i], w_out[i], ln_ffn[i], w_up[i], conv_w[i], conv_b[i], w_down[i],
                ln_ple[i], w_ple_gate[i], w_ple[i], ln_final, tm=512)
```

```python
import functools

import jax
import jax.numpy as jnp
from jax import lax
from jax.experimental import pallas as pl
from jax.experimental.pallas import tpu as pltpu

D_MODEL = 1024
HEAD_DIM = 64
ATT_WIDTH = 512
N_ATT_HEADS = ATT_WIDTH // HEAD_DIM
DIL_PATTERNS = ((128, 1), (512, 4), (2048, 16))
SPAN = 128
POOL_WIDTH = 512
POOL_WINDOWS = (2, 4, 8, 16)
POOL_GROUP_DIM = 128
MAX_POOL_WINDOW = 16
IN_WIDTH = 3 * ATT_WIDTH + POOL_WIDTH
D_FF = 2816
CONV_WIDTH = 3
PLE_DIM = 256
EPS = 1e-6

LANES = 128
SUBLANES = 8
HEADS_PER_SLAB = LANES // HEAD_DIM
VMEM_LIMIT_BYTES = 56 * 1024 * 1024

F32 = jnp.float32
BF16 = jnp.bfloat16
U32 = jnp.uint32


def _pack_rows(t):
    return pltpu.bitcast(t.astype(BF16), U32)


def _unpack_rows(t):
    return pltpu.bitcast(t, BF16)


def _rms_scale(t, g):
    inv = lax.rsqrt(jnp.mean(t * t, axis=-1, keepdims=True) + EPS)
    return t * inv * g


QKV_WIDTH = 3 * ATT_WIDTH
LOG2E = 1.4426950408889634
Q_SCALE = (HEAD_DIM ** -0.5) * LOG2E
N_QKV_SLABS = QKV_WIDTH // LANES


def _in_proj_body(x_ref, g_ref, w_ref, pw_ref, ps_ref, nat_ref, r4_ref, r16_ref, pool_ref, zbuf, z4buf,
                  ucarry, *, tm):
    s_idx = pl.program_id(1)
    hw = MAX_POOL_WINDOW

    @pl.when(s_idx == 0)
    def _():
        ucarry[...] = jnp.zeros_like(ucarry)

    hn = _rms_scale(x_ref[...], g_ref[...]).astype(BF16)
    a = ATT_WIDTH

    u = jnp.dot(hn, w_ref[:, 3 * a:], preferred_element_type=F32)
    e = jnp.concatenate([ucarry[...], u], axis=0)
    ucarry[...] = u[tm - hw:, :]
    gd = POOL_GROUP_DIM
    s2 = e + pltpu.roll(e, 1, 0)
    s4 = s2[:, gd:] + pltpu.roll(s2[:, gd:], 2, 0)
    s8 = s4[:, gd:] + pltpu.roll(s4[:, gd:], 4, 0)
    s16 = s8[:, gd:] + pltpu.roll(s8[:, gd:], 8, 0)
    sums = (s2[hw:, 0:gd], s4[hw:, 0:gd], s8[hw:, 0:gd], s16[hw:, :])
    t1 = (s_idx * tm + lax.broadcasted_iota(jnp.int32, (tm, 1), 0) + 1).astype(F32)
    dlts = []
    for g, w in enumerate(POOL_WINDOWS):
        inv_cnt = 1.0 / jnp.minimum(t1, float(w))
        dlts.append((sums[g] * inv_cnt - u[:, g * gd:(g + 1) * gd]).astype(BF16))

    slabs_per_part = a // LANES
    n4 = tm // 4
    for part in range(3):
        z = jnp.dot(hn, w_ref[:, part * a:(part + 1) * a], preferred_element_type=F32)
        if part == 0:
            z = z * Q_SCALE
        nat_ref[:, part * a:(part + 1) * a] = _pack_rows(z)
        for sl in range(slabs_per_part):
            j = part * slabs_per_part + sl
            cols = slice(j * LANES, (j + 1) * LANES)
            zbuf[j] = z[:, sl * LANES:(sl + 1) * LANES]
            for c in range(4):
                cls4 = zbuf[j, pl.ds(c, n4, stride=4), :]
                r4_ref[c, :, cols] = _pack_rows(cls4)
                z4buf[j, c * n4:(c + 1) * n4, :] = cls4
            for c in range(4):
                for b in range(4):
                    cls16 = z4buf[j, pl.ds(c * n4 + b, tm // 16, stride=4), :]
                    r16_ref[4 * b + c, :, cols] = _pack_rows(cls16)

    for g in range(len(POOL_WINDOWS)):
        y = jnp.dot(dlts[g], pw_ref[g], preferred_element_type=F32)
        pool_ref[:, g * gd:(g + 1) * gd] = _pack_rows(y * ps_ref[:, g * gd:(g + 1) * gd])


def _in_proj(x, ln_mix, w_in, pool_w, pool_scale, *, tm):
    B, S, D = x.shape
    grid = (B, S // tm)
    row = lambda b, s: (b, s, 0)
    cls_row = lambda b, s: (b, 0, s, 0)
    const2 = lambda b, s: (0, 0)
    const3 = lambda b, s: (0, 0, 0)
    nat, r4, r16, pool = pl.pallas_call(
        functools.partial(_in_proj_body, tm=tm),
        grid=grid,
        in_specs=[
            pl.BlockSpec((None, tm, D), row),
            pl.BlockSpec((1, D), const2),
            pl.BlockSpec((D, IN_WIDTH), const2),
            pl.BlockSpec((len(POOL_WINDOWS), POOL_GROUP_DIM, POOL_GROUP_DIM), const3),
            pl.BlockSpec((1, POOL_WIDTH), const2),
        ],
        out_specs=[
            pl.BlockSpec((None, tm // 2, QKV_WIDTH), row),
            pl.BlockSpec((None, 4, tm // 8, QKV_WIDTH), cls_row),
            pl.BlockSpec((None, 16, tm // 32, QKV_WIDTH), cls_row),
            pl.BlockSpec((None, tm // 2, POOL_WIDTH), row),
        ],
        out_shape=[
            jax.ShapeDtypeStruct((B, S // 2, QKV_WIDTH), U32),
            jax.ShapeDtypeStruct((B, 4, S // 8, QKV_WIDTH), U32),
            jax.ShapeDtypeStruct((B, 16, S // 32, QKV_WIDTH), U32),
            jax.ShapeDtypeStruct((B, S // 2, POOL_WIDTH), U32),
        ],
        scratch_shapes=[
            pltpu.VMEM((N_QKV_SLABS, tm, LANES), F32),
            pltpu.VMEM((N_QKV_SLABS, tm, LANES), F32),
            pltpu.VMEM((MAX_POOL_WINDOW, POOL_WIDTH), F32),
        ],
        compiler_params=pltpu.CompilerParams(
            dimension_semantics=("arbitrary", "arbitrary"),
            vmem_limit_bytes=VMEM_LIMIT_BYTES),
        name="in_proj_pool",
    )(x, ln_mix.reshape(1, D), w_in.astype(BF16), pool_w.astype(BF16), pool_scale.reshape(1, POOL_WIDTH))
    return nat, r4.reshape(B, S // 2, QKV_WIDTH), r16.reshape(B, S // 2, QKV_WIDTH), pool


N_BIAS_VARIANTS = 3


def _attn_body(slopes_ref, qn_ref, kn_ref, vn_ref, q4_ref, k4_ref, v4_ref, q16_ref, k16_ref, v16_ref,
               o_ref, bias_s, m_s, d_s, u_s, *, seq):
    hp = pl.program_id(1)
    n_blocks = seq // SPAN
    half = SPAN // 2
    lane = lax.broadcasted_iota(jnp.int32, (1, LANES), 1)
    is_h0 = lane < HEAD_DIM
    head_mask = (is_h0, jnp.logical_not(is_h0))
    head_sel = tuple(jnp.where(hm, 1.0, 0.0).astype(BF16) for hm in head_mask)
    ones_cols = tuple(jnp.broadcast_to(sel, (2 * SPAN, LANES)) for sel in head_sel)

    qi = lax.broadcasted_iota(jnp.int32, (SPAN, 2 * SPAN), 0)
    kj = lax.broadcasted_iota(jnp.int32, (SPAN, 2 * SPAN), 1)
    diffs = (qi - kj, qi + SPAN - kj, qi + SPAN - kj)
    extra = (None, kj >= SPAN, None)
    for p, (_, dil) in enumerate(DIL_PATTERNS):
        for hh in range(HEADS_PER_SLAB):
            slope = slopes_ref[hp * HEADS_PER_SLAB + hh]
            for var in range(N_BIAS_VARIANTS):
                diff = diffs[var]
                valid = (diff >= 0) & (diff <= SPAN)
                if extra[var] is not None:
                    valid = valid & extra[var]
                penalty = (-(slope * (float(dil) * LOG2E))) * diff.astype(F32)
                bias_s[p, var, hh] = jnp.where(valid, penalty, -jnp.inf)

    branch_refs = ((qn_ref, kn_ref, vn_ref), (q4_ref, k4_ref, v4_ref), (q16_ref, k16_ref, v16_ref))
    for p in (2, 1, 0):
        dil = DIL_PATTERNS[p][1]
        blocks_per_class = n_blocks // dil
        q_ref, k_ref, v_ref = branch_refs[p]

        for g in range(n_blocks):
            cls, jb = divmod(g, blocks_per_class)
            variant = 0 if g == 0 else (1 if jb == 0 else 2)
            row0 = g * half
            win0 = max(g - 1, 0) * half
            qb = _unpack_rows(q_ref[row0:row0 + half, :])
            kw = _unpack_rows(k_ref[win0:win0 + 2 * half, :])
            vw = _unpack_rows(v_ref[win0:win0 + 2 * half, :])
            ms, es, vs = [], [], []
            for hh in range(HEADS_PER_SLAB):
                qh = qb * head_sel[hh]
                s = lax.dot_general(qh, kw, (((1,), (1,)), ((), ())), preferred_element_type=F32)
                s = s + bias_s[p, variant, hh]
                m = jnp.max(s, axis=1, keepdims=True)
                ms.append(m)
                es.append(jnp.exp2(s - m).astype(BF16))
                vs.append(jnp.concatenate(
                    [jnp.where(head_mask[hh], vw, jnp.zeros_like(vw)), ones_cols[hh]], axis=1))
            ud = jnp.dot(jnp.concatenate(es, axis=1), jnp.concatenate(vs, axis=0),
                         preferred_element_type=F32)
            m_new = jnp.where(is_h0, ms[0], ms[1])
            u_new = ud[:, 0:LANES]
            d_new = ud[:, LANES:]
            start = (dil * SPAN) * jb + cls
            rows = pl.ds(start, SPAN) if dil == 1 else pl.ds(start, SPAN, stride=dil)
            if p == 2:
                m_s[rows, :] = m_new
                d_s[rows, :] = d_new
                u_s[rows, :] = u_new
            else:
                m_old = m_s[rows, :]
                m_all = jnp.maximum(m_old, m_new)
                w_old = jnp.exp2(m_old - m_all)
                w_new = jnp.exp2(m_new - m_all)
                d_all = d_s[rows, :] * w_old + d_new * w_new
                u_all = u_s[rows, :] * w_old + u_new * w_new
                if p == 1:
                    m_s[rows, :] = m_all
                    d_s[rows, :] = d_all
                    u_s[rows, :] = u_all
                else:
                    o_ref[row0:row0 + half, :] = _pack_rows(u_all / d_all)


def _attention(qkv_nat, qkv_r4, qkv_r16, slopes):
    B, half_s, _ = qkv_nat.shape
    S = 2 * half_s
    n_slabs = ATT_WIDTH // LANES

    def slab(part):
        return pl.BlockSpec((None, half_s, LANES), lambda b, h, slopes_ref: (b, 0, part * n_slabs + h))

    qkv_specs = [slab(0), slab(1), slab(2)]
    n_br = len(DIL_PATTERNS)
    return pl.pallas_call(
        functools.partial(_attn_body, seq=S),
        grid_spec=pltpu.PrefetchScalarGridSpec(
            num_scalar_prefetch=1,
            grid=(B, n_slabs),
            in_specs=qkv_specs * n_br,
            out_specs=pl.BlockSpec((None, half_s, LANES), lambda b, h, slopes_ref: (b, 0, h)),
            scratch_shapes=[
                pltpu.VMEM((n_br, N_BIAS_VARIANTS, HEADS_PER_SLAB, SPAN, 2 * SPAN), F32),
                pltpu.VMEM((S, LANES), F32),
                pltpu.VMEM((S, LANES), F32),
                pltpu.VMEM((S, LANES), F32),
            ],
        ),
        out_shape=jax.ShapeDtypeStruct((B, half_s, ATT_WIDTH), U32),
        compiler_params=pltpu.CompilerParams(
            dimension_semantics=("arbitrary", "arbitrary"),
            vmem_limit_bytes=VMEM_LIMIT_BYTES),
        name="dilated_attention",
    )(slopes, qkv_nat, qkv_nat, qkv_nat, qkv_r4, qkv_r4, qkv_r4, qkv_r16, qkv_r16, qkv_r16)


FF_CHUNK = 256
N_STAGE_SLOTS = 4


def _ffn_body(x_ref, att_ref, pool_ref, p_ref, wout_ref, lnf_ref, wup_ref, cw_ref, cb_ref, wdn_ref,
              lnp_ref, wg_ref, wple_ref, lnfin_ref, o_ref, act_s, carry_s, stage_s, *, tm):
    s_idx = pl.program_id(1)

    @pl.when(s_idx == 0)
    def _():
        carry_s[...] = jnp.zeros_like(carry_s)

    h = x_ref[...]
    h = h + jnp.dot(_unpack_rows(att_ref[...]), wout_ref[0:ATT_WIDTH, :], preferred_element_type=F32)
    h = h + jnp.dot(_unpack_rows(pool_ref[...]), wout_ref[ATT_WIDTH:, :], preferred_element_type=F32)

    hn = _rms_scale(h, lnf_ref[...]).astype(BF16)

    def conv(c0, slot):
        cols = slice(c0, c0 + FF_CHUNK)
        pre = jnp.dot(hn, wup_ref[:, cols], preferred_element_type=F32)
        sh1, sh2 = [], []
        for sl in range(FF_CHUNK // LANES):
            lanes = slice(sl * LANES, (sl + 1) * LANES)
            stage_s[slot, sl, pl.ds(0, SUBLANES, stride=2), :] = carry_s[:, c0 + sl * LANES:c0 + (sl + 1) * LANES]
            stage_s[slot, sl, pl.ds(2 * SUBLANES, tm, stride=2), :] = pre[:, lanes]
            sh1.append(stage_s[slot, sl, pl.ds(2 * SUBLANES - 2, tm, stride=2), :])
            sh2.append(stage_s[slot, sl, pl.ds(2 * SUBLANES - 4, tm, stride=2), :])
        carry_s[:, cols] = pre[tm - SUBLANES:, :]
        y = cb_ref[:, cols] + cw_ref[0:1, cols] * jnp.concatenate(sh2, axis=1)
        y = y + cw_ref[1:2, cols] * jnp.concatenate(sh1, axis=1)
        return y + cw_ref[2:3, cols] * pre

    for c in range(D_FF // FF_CHUNK):
        c0 = c * FF_CHUNK
        gate = conv(c0, (2 * c) % N_STAGE_SLOTS)
        val = conv(D_FF + c0, (2 * c + 1) % N_STAGE_SLOTS)
        act_s[:, c0:c0 + FF_CHUNK] = (gate * jax.nn.sigmoid(gate) * val).astype(BF16)

    h = h + jnp.dot(act_s[...], wdn_ref[...], preferred_element_type=F32)

    hn = _rms_scale(h, lnp_ref[...]).astype(BF16)
    g = jax.nn.sigmoid(jnp.dot(hn, wg_ref[...], preferred_element_type=F32))
    emb = jnp.dot(p_ref[...].astype(BF16), wple_ref[...], preferred_element_type=F32)
    h = h + g * emb
    o_ref[...] = _rms_scale(h, lnfin_ref[...])


def _ffn(x, att, pool, p, w_out, ln_ffn, w_up, conv_w, conv_b, w_down, ln_ple, w_ple_gate, w_ple,
         ln_final, *, tm):
    B, S, D = x.shape
    grid = (B, S // tm)
    row = lambda b, s: (b, s, 0)
    const2 = lambda b, s: (0, 0)

    def resident(shape):
        return pl.BlockSpec(shape, const2, pipeline_mode=pl.Buffered(1))

    return pl.pallas_call(
        functools.partial(_ffn_body, tm=tm),
        grid=grid,
        in_specs=[
            pl.BlockSpec((None, tm, D), row),
            pl.BlockSpec((None, tm // 2, ATT_WIDTH), row),
            pl.BlockSpec((None, tm // 2, POOL_WIDTH), row),
            pl.BlockSpec((None, tm, PLE_DIM), row),
            resident((ATT_WIDTH + POOL_WIDTH, D)),
            resident((1, D)),
            resident((D, 2 * D_FF)),
            resident((CONV_WIDTH, 2 * D_FF)),
            resident((1, 2 * D_FF)),
            resident((D_FF, D)),
            resident((1, D)),
            resident((D, D)),
            resident((PLE_DIM, D)),
            resident((1, D)),
        ],
        out_specs=pl.BlockSpec((None, tm, D), row),
        out_shape=jax.ShapeDtypeStruct((B, S, D), F32),
        scratch_shapes=[
            pltpu.VMEM((tm, D_FF), BF16),
            pltpu.VMEM((SUBLANES, 2 * D_FF), F32),
            pltpu.VMEM((N_STAGE_SLOTS, FF_CHUNK // LANES, 2 * (SUBLANES + tm), LANES), F32),
        ],
        compiler_params=pltpu.CompilerParams(
            dimension_semantics=("arbitrary", "arbitrary"),
            vmem_limit_bytes=VMEM_LIMIT_BYTES),
        name="outproj_ffn_ple",
    )(x, att, pool, p, w_out.astype(BF16), ln_ffn.reshape(1, D), w_up.astype(BF16), conv_w,
      conv_b.reshape(1, 2 * D_FF), w_down.astype(BF16), ln_ple.reshape(1, D), w_ple_gate.astype(BF16),
      w_ple.astype(BF16), ln_final.reshape(1, D))


def kernel(x, p, ln_mix, w_in, pool_w, pool_scale, w_out, ln_ffn, w_up, conv_w, conv_b, w_down, ln_ple,
           w_ple_gate, w_ple, ln_final):
    depth = p.shape[0]
    slopes = jnp.exp2(-8.0 * (jnp.arange(N_ATT_HEADS, dtype=F32) + 1.0) / N_ATT_HEADS)
    assert depth == 1, "the final RMSNorm is fused into the last layer's kernel"
    i = 0
    qkv_nat, qkv_r4, qkv_r16, pool = _in_proj(x, ln_mix[i], w_in[i], pool_w[i], pool_scale[i], tm=512)
    att = _attention(qkv_nat, qkv_r4, qkv_r16, slopes)
    return _ffn(x, att, pool, p[i], w_out[i], ln_ffn[i], w_up[i], conv_w[i], conv_b[i], w_down[i],
                ln_ple[i], w_ple_gate[i], w_ple[i], ln_final, tm=512)
```

```python
import functools

import jax
import jax.numpy as jnp
from jax import lax
from jax.experimental import pallas as pl
from jax.experimental.pallas import tpu as pltpu

D_MODEL = 1024
HEAD_DIM = 64
ATT_WIDTH = 512
N_ATT_HEADS = ATT_WIDTH // HEAD_DIM
DIL_PATTERNS = ((128, 1), (512, 4), (2048, 16))
SPAN = 128
POOL_WIDTH = 512
POOL_WINDOWS = (2, 4, 8, 16)
POOL_GROUP_DIM = 128
MAX_POOL_WINDOW = 16
IN_WIDTH = 3 * ATT_WIDTH + POOL_WIDTH
D_FF = 2816
CONV_WIDTH = 3
PLE_DIM = 256
EPS = 1e-6

LANES = 128
SUBLANES = 8
HEADS_PER_SLAB = LANES // HEAD_DIM
VMEM_LIMIT_BYTES = 56 * 1024 * 1024

F32 = jnp.float32
BF16 = jnp.bfloat16
U32 = jnp.uint32


def _pack_rows(t):
    return pltpu.bitcast(t.astype(BF16), U32)


def _unpack_rows(t):
    return pltpu.bitcast(t, BF16)


def _rms_scale(t, g):
    inv = lax.rsqrt(jnp.mean(t * t, axis=-1, keepdims=True) + EPS)
    return t * inv * g


QKV_WIDTH = 3 * ATT_WIDTH
LOG2E = 1.4426950408889634
Q_SCALE = (HEAD_DIM ** -0.5) * LOG2E
N_QKV_SLABS = QKV_WIDTH // LANES
SLAB_QKV = 3 * LANES


def _in_proj_body(x_ref, g_ref, w_ref, pw_ref, ps_ref, nat_ref, r4_ref, r16_ref, pool_ref, zbuf, z4buf,
                  ucarry, *, tm):
    s_idx = pl.program_id(1)
    hw = MAX_POOL_WINDOW

    @pl.when(s_idx == 0)
    def _():
        ucarry[...] = jnp.zeros_like(ucarry)

    hn = _rms_scale(x_ref[...], g_ref[...]).astype(BF16)
    a = ATT_WIDTH

    u = jnp.dot(hn, w_ref[:, 3 * a:], preferred_element_type=F32)
    e = jnp.concatenate([ucarry[...], u], axis=0)
    ucarry[...] = u[tm - hw:, :]
    gd = POOL_GROUP_DIM
    s2 = e + pltpu.roll(e, 1, 0)
    s4 = s2[:, gd:] + pltpu.roll(s2[:, gd:], 2, 0)
    s8 = s4[:, gd:] + pltpu.roll(s4[:, gd:], 4, 0)
    s16 = s8[:, gd:] + pltpu.roll(s8[:, gd:], 8, 0)
    sums = (s2[hw:, 0:gd], s4[hw:, 0:gd], s8[hw:, 0:gd], s16[hw:, :])
    t1 = (s_idx * tm + lax.broadcasted_iota(jnp.int32, (tm, 1), 0) + 1).astype(F32)
    dlts = []
    for g, w in enumerate(POOL_WINDOWS):
        inv_cnt = 1.0 / jnp.minimum(t1, float(w))
        dlts.append((sums[g] * inv_cnt - u[:, g * gd:(g + 1) * gd]).astype(BF16))

    slabs_per_part = a // LANES
    n4 = tm // 4
    for part in range(3):
        z = jnp.dot(hn, w_ref[:, part * a:(part + 1) * a], preferred_element_type=F32)
        if part == 0:
            z = z * Q_SCALE
        cols = slice(part * LANES, (part + 1) * LANES)
        for sl in range(slabs_per_part):
            j = part * slabs_per_part + sl
            zs = z[:, sl * LANES:(sl + 1) * LANES]
            nat_ref[sl, :, cols] = _pack_rows(zs)
            zbuf[j] = zs
            for c in range(4):
                cls4 = zbuf[j, pl.ds(c, n4, stride=4), :]
                r4_ref[sl, c, :, cols] = _pack_rows(cls4)
                z4buf[j, c * n4:(c + 1) * n4, :] = cls4
            for c in range(4):
                for b in range(4):
                    cls16 = z4buf[j, pl.ds(c * n4 + b, tm // 16, stride=4), :]
                    r16_ref[sl, 4 * b + c, :, cols] = _pack_rows(cls16)

    for g in range(len(POOL_WINDOWS)):
        y = jnp.dot(dlts[g], pw_ref[g], preferred_element_type=F32)
        pool_ref[:, g * gd:(g + 1) * gd] = _pack_rows(y * ps_ref[:, g * gd:(g + 1) * gd])


def _in_proj(x, ln_mix, w_in, pool_w, pool_scale, *, tm):
    B, S, D = x.shape
    grid = (B, S // tm)
    row = lambda b, s: (b, s, 0)
    slab_row = lambda b, s: (b, 0, s, 0)
    slab_cls_row = lambda b, s: (b, 0, 0, s, 0)
    n_slabs = ATT_WIDTH // LANES
    const2 = lambda b, s: (0, 0)
    const3 = lambda b, s: (0, 0, 0)
    nat, r4, r16, pool = pl.pallas_call(
        functools.partial(_in_proj_body, tm=tm),
        grid=grid,
        in_specs=[
            pl.BlockSpec((None, tm, D), row),
            pl.BlockSpec((1, D), const2),
            pl.BlockSpec((D, IN_WIDTH), const2),
            pl.BlockSpec((len(POOL_WINDOWS), POOL_GROUP_DIM, POOL_GROUP_DIM), const3),
            pl.BlockSpec((1, POOL_WIDTH), const2),
        ],
        out_specs=[
            pl.BlockSpec((None, n_slabs, tm // 2, SLAB_QKV), slab_row),
            pl.BlockSpec((None, n_slabs, 4, tm // 8, SLAB_QKV), slab_cls_row),
            pl.BlockSpec((None, n_slabs, 16, tm // 32, SLAB_QKV), slab_cls_row),
            pl.BlockSpec((None, tm // 2, POOL_WIDTH), row),
        ],
        out_shape=[
            jax.ShapeDtypeStruct((B, n_slabs, S // 2, SLAB_QKV), U32),
            jax.ShapeDtypeStruct((B, n_slabs, 4, S // 8, SLAB_QKV), U32),
            jax.ShapeDtypeStruct((B, n_slabs, 16, S // 32, SLAB_QKV), U32),
            jax.ShapeDtypeStruct((B, S // 2, POOL_WIDTH), U32),
        ],
        scratch_shapes=[
            pltpu.VMEM((N_QKV_SLABS, tm, LANES), F32),
            pltpu.VMEM((N_QKV_SLABS, tm, LANES), F32),
            pltpu.VMEM((MAX_POOL_WINDOW, POOL_WIDTH), F32),
        ],
        compiler_params=pltpu.CompilerParams(
            dimension_semantics=("arbitrary", "arbitrary"),
            vmem_limit_bytes=VMEM_LIMIT_BYTES),
        name="in_proj_pool",
    )(x, ln_mix.reshape(1, D), w_in.astype(BF16), pool_w.astype(BF16), pool_scale.reshape(1, POOL_WIDTH))
    flat = (B, n_slabs, S // 2, SLAB_QKV)
    return nat, r4.reshape(flat), r16.reshape(flat), pool


N_BIAS_VARIANTS = 3


def _attn_body(slopes_ref, nat_ref, r4_ref, r16_ref, o_ref, bias_s, m_s, d_s, u_s, *, seq):
    hp = pl.program_id(1)
    n_blocks = seq // SPAN
    half = SPAN // 2
    lane = lax.broadcasted_iota(jnp.int32, (1, LANES), 1)
    is_h0 = lane < HEAD_DIM
    head_mask = (is_h0, jnp.logical_not(is_h0))
    head_sel = tuple(jnp.where(hm, 1.0, 0.0).astype(BF16) for hm in head_mask)
    ones_cols = tuple(jnp.broadcast_to(sel, (2 * SPAN, LANES)) for sel in head_sel)

    qi = lax.broadcasted_iota(jnp.int32, (SPAN, 2 * SPAN), 0)
    kj = lax.broadcasted_iota(jnp.int32, (SPAN, 2 * SPAN), 1)
    diffs = (qi - kj, qi + SPAN - kj, qi + SPAN - kj)
    extra = (None, kj >= SPAN, None)
    for p, (_, dil) in enumerate(DIL_PATTERNS):
        for hh in range(HEADS_PER_SLAB):
            slope = slopes_ref[hp * HEADS_PER_SLAB + hh]
            for var in range(N_BIAS_VARIANTS):
                diff = diffs[var]
                valid = (diff >= 0) & (diff <= SPAN)
                if extra[var] is not None:
                    valid = valid & extra[var]
                penalty = (-(slope * (float(dil) * LOG2E))) * diff.astype(F32)
                bias_s[p, var, hh] = jnp.where(valid, penalty, -jnp.inf)

    branch_refs = (nat_ref, r4_ref, r16_ref)

    def block(p, g):
        dil = DIL_PATTERNS[p][1]
        qkv_ref = branch_refs[p]
        cls, jb = divmod(g, n_blocks // dil)
        variant = 0 if g == 0 else (1 if jb == 0 else 2)
        row0 = g * half
        win0 = max(g - 1, 0) * half
        qb = _unpack_rows(qkv_ref[row0:row0 + half, 0:LANES])
        kw = _unpack_rows(qkv_ref[win0:win0 + 2 * half, LANES:2 * LANES])
        vw = _unpack_rows(qkv_ref[win0:win0 + 2 * half, 2 * LANES:3 * LANES])
        ms, es, vs = [], [], []
        for hh in range(HEADS_PER_SLAB):
            qh = qb * head_sel[hh]
            s = lax.dot_general(qh, kw, (((1,), (1,)), ((), ())), preferred_element_type=F32)
            s = s + bias_s[p, variant, hh]
            m = jnp.max(s, axis=1, keepdims=True)
            ms.append(m)
            es.append(jnp.exp2(s - m).astype(BF16))
            vs.append(jnp.concatenate(
                [jnp.where(head_mask[hh], vw, jnp.zeros_like(vw)), ones_cols[hh]], axis=1))
        ud = jnp.dot(jnp.concatenate(es, axis=1), jnp.concatenate(vs, axis=0),
                     preferred_element_type=F32)
        m_new = jnp.where(is_h0, ms[0], ms[1])
        u_new = ud[:, 0:LANES]
        d_new = ud[:, LANES:]
        start = (dil * SPAN) * jb + cls
        rows = pl.ds(start, SPAN) if dil == 1 else pl.ds(start, SPAN, stride=dil)
        if p == 2:
            m_s[rows, :] = m_new
            d_s[rows, :] = d_new
            u_s[rows, :] = u_new
            return
        m_old = m_s[rows, :]
        m_all = jnp.maximum(m_old, m_new)
        w_old = jnp.exp2(m_old - m_all)
        w_new = jnp.exp2(m_new - m_all)
        d_all = d_s[rows, :] * w_old + d_new * w_new
        u_all = u_s[rows, :] * w_old + u_new * w_new
        if p == 1:
            m_s[rows, :] = m_all
            d_s[rows, :] = d_all
            u_s[rows, :] = u_all
        else:
            o_ref[row0:row0 + half, :] = _pack_rows(u_all / d_all)

    for p, g in _block_order(n_blocks):
        block(p, g)


def _block_order(n_blocks):
    n16, n4 = n_blocks // 16, n_blocks // 4
    s16 = [(4 * b + c) * n16 + jb for jb in range(n16) for c in range(4) for b in range(4)]
    s4 = [c * n4 + jb for jb in range(n4) for c in range(4)]
    s1 = list(range(n_blocks))

    def deps_ok(p, g, done):
        if p == 2:
            return True
        if p == 1:
            c, jb = divmod(g, n4)
            jb16 = (4 * jb) // 16
            return all((2, (4 * b + c) * n16 + jb16) in done for b in range(4))
        return all((1, c * n4 + g // 4) in done for c in range(4))

    streams = {2: s16, 1: s4, 0: s1}
    pos = {2: 0, 1: 0, 0: 0}
    done, order = set(), []
    while len(order) < 3 * n_blocks:
        for p in (2, 1, 0):
            if pos[p] < len(streams[p]) and deps_ok(p, streams[p][pos[p]], done):
                unit = (p, streams[p][pos[p]])
                pos[p] += 1
                done.add(unit)
                order.append(unit)
    return order


def _attention(qkv_nat, qkv_r4, qkv_r16, slopes):
    B, n_slabs, half_s, _ = qkv_nat.shape
    S = 2 * half_s
    qkv_spec = pl.BlockSpec((None, None, half_s, SLAB_QKV), lambda b, h, slopes_ref: (b, h, 0, 0))
    n_br = len(DIL_PATTERNS)
    return pl.pallas_call(
        functools.partial(_attn_body, seq=S),
        grid_spec=pltpu.PrefetchScalarGridSpec(
            num_scalar_prefetch=1,
            grid=(B, n_slabs),
            in_specs=[qkv_spec] * n_br,
            out_specs=pl.BlockSpec((None, half_s, LANES), lambda b, h, slopes_ref: (b, 0, h)),
            scratch_shapes=[
                pltpu.VMEM((n_br, N_BIAS_VARIANTS, HEADS_PER_SLAB, SPAN, 2 * SPAN), F32),
                pltpu.VMEM((S, LANES), F32),
                pltpu.VMEM((S, LANES), F32),
                pltpu.VMEM((S, LANES), F32),
            ],
        ),
        out_shape=jax.ShapeDtypeStruct((B, half_s, ATT_WIDTH), U32),
        compiler_params=pltpu.CompilerParams(
            dimension_semantics=("arbitrary", "arbitrary"),
            vmem_limit_bytes=VMEM_LIMIT_BYTES),
        name="dilated_attention",
    )(slopes, qkv_nat, qkv_r4, qkv_r16)


FF_CHUNK = 256
N_STAGE_SLOTS = 4


def _ffn_body(x_ref, att_ref, pool_ref, p_ref, wout_ref, lnf_ref, wup_ref, cw_ref, cb_ref, wdn_ref,
              lnp_ref, wg_ref, wple_ref, lnfin_ref, o_ref, act_s, carry_s, stage_s, *, tm):
    s_idx = pl.program_id(1)

    @pl.when(s_idx == 0)
    def _():
        carry_s[...] = jnp.zeros_like(carry_s)

    h = x_ref[...]
    h = h + jnp.dot(_unpack_rows(att_ref[...]), wout_ref[0:ATT_WIDTH, :], preferred_element_type=F32)
    h = h + jnp.dot(_unpack_rows(pool_ref[...]), wout_ref[ATT_WIDTH:, :], preferred_element_type=F32)

    hn = _rms_scale(h, lnf_ref[...]).astype(BF16)

    def conv(c0, slot):
        cols = slice(c0, c0 + FF_CHUNK)
        pre = jnp.dot(hn, wup_ref[:, cols], preferred_element_type=F32)
        sh1, sh2 = [], []
        for sl in range(FF_CHUNK // LANES):
            lanes = slice(sl * LANES, (sl + 1) * LANES)
            stage_s[slot, sl, pl.ds(0, SUBLANES, stride=2), :] = carry_s[:, c0 + sl * LANES:c0 + (sl + 1) * LANES]
            stage_s[slot, sl, pl.ds(2 * SUBLANES, tm, stride=2), :] = pre[:, lanes]
            sh1.append(stage_s[slot, sl, pl.ds(2 * SUBLANES - 2, tm, stride=2), :])
            sh2.append(stage_s[slot, sl, pl.ds(2 * SUBLANES - 4, tm, stride=2), :])
        carry_s[:, cols] = pre[tm - SUBLANES:, :]
        y = cb_ref[:, cols] + cw_ref[0:1, cols] * jnp.concatenate(sh2, axis=1)
        y = y + cw_ref[1:2, cols] * jnp.concatenate(sh1, axis=1)
        return y + cw_ref[2:3, cols] * pre

    emb = jnp.dot(p_ref[...].astype(BF16), wple_ref[...], preferred_element_type=F32)

    for c in range(D_FF // FF_CHUNK):
        c0 = c * FF_CHUNK
        gate = conv(c0, (2 * c) % N_STAGE_SLOTS)
        val = conv(D_FF + c0, (2 * c + 1) % N_STAGE_SLOTS)
        act_s[:, c0:c0 + FF_CHUNK] = (gate * jax.nn.sigmoid(gate) * val).astype(BF16)

    th = tm // 2
    h_halves = []
    for half in range(2):
        rows = slice(half * th, (half + 1) * th)
        h_halves.append(h[rows, :] + jnp.dot(act_s[rows, :], wdn_ref[...], preferred_element_type=F32))
    for half in range(2):
        rows = slice(half * th, (half + 1) * th)
        hh = h_halves[half]
        hn3 = _rms_scale(hh, lnp_ref[...]).astype(BF16)
        g = jax.nn.sigmoid(jnp.dot(hn3, wg_ref[...], preferred_element_type=F32))
        hh = hh + g * emb[rows, :]
        o_ref[rows, :] = _rms_scale(hh, lnfin_ref[...])


def _ffn(x, att, pool, p, w_out, ln_ffn, w_up, conv_w, conv_b, w_down, ln_ple, w_ple_gate, w_ple,
         ln_final, *, tm):
    B, S, D = x.shape
    grid = (B, S // tm)
    row = lambda b, s: (b, s, 0)
    const2 = lambda b, s: (0, 0)

    def resident(shape):
        return pl.BlockSpec(shape, const2, pipeline_mode=pl.Buffered(1))

    return pl.pallas_call(
        functools.partial(_ffn_body, tm=tm),
        grid=grid,
        in_specs=[
            pl.BlockSpec((None, tm, D), row),
            pl.BlockSpec((None, tm // 2, ATT_WIDTH), row),
            pl.BlockSpec((None, tm // 2, POOL_WIDTH), row),
            pl.BlockSpec((None, tm, PLE_DIM), row),
            resident((ATT_WIDTH + POOL_WIDTH, D)),
            resident((1, D)),
            resident((D, 2 * D_FF)),
            resident((CONV_WIDTH, 2 * D_FF)),
            resident((1, 2 * D_FF)),
            resident((D_FF, D)),
            resident((1, D)),
            resident((D, D)),
            resident((PLE_DIM, D)),
            resident((1, D)),
        ],
        out_specs=pl.BlockSpec((None, tm, D), row),
        out_shape=jax.ShapeDtypeStruct((B, S, D), F32),
        scratch_shapes=[
            pltpu.VMEM((tm, D_FF), BF16),
            pltpu.VMEM((SUBLANES, 2 * D_FF), F32),
            pltpu.VMEM((N_STAGE_SLOTS, FF_CHUNK // LANES, 2 * (SUBLANES + tm), LANES), F32),
        ],
        compiler_params=pltpu.CompilerParams(
            dimension_semantics=("arbitrary", "arbitrary"),
            vmem_limit_bytes=VMEM_LIMIT_BYTES),
        name="outproj_ffn_ple",
    )(x, att, pool, p, w_out.astype(BF16), ln_ffn.reshape(1, D), w_up.astype(BF16), conv_w,
      conv_b.reshape(1, 2 * D_FF), w_down.astype(BF16), ln_ple.reshape(1, D), w_ple_gate.astype(BF16),
      w_ple.astype(BF16), ln_final.reshape(1, D))


def kernel(x, p, ln_mix, w_in, pool_w, pool_scale, w_out, ln_ffn, w_up, conv_w, conv_b, w_down, ln_ple,
           w_ple_gate, w_ple, ln_final):
    depth = p.shape[0]
    slopes = jnp.exp2(-8.0 * (jnp.arange(N_ATT_HEADS, dtype=F32) + 1.0) / N_ATT_HEADS)
    assert depth == 1, "the final RMSNorm is fused into the last layer's kernel"
    i = 0
    qkv_nat, qkv_r4, qkv_r16, pool = _in_proj(x, ln_mix[i], w_in[i], pool_w[i], pool_scale[i], tm=512)
    att = _attention(qkv_nat, qkv_r4, qkv_r16, slopes)
    return _ffn(x, att, pool, p[i], w_out[i], ln_ffn[i], w_up[i], conv_w[i], conv_b[i], w_down[i],
                ln_ple[i], w_ple_gate[i], w_ple[i], ln_final, tm=512)
```

```python
import functools

import jax
import jax.numpy as jnp
from jax import lax
from jax.experimental import pallas as pl
from jax.experimental.pallas import tpu as pltpu

D_MODEL = 1024
HEAD_DIM = 64
ATT_WIDTH = 512
N_ATT_HEADS = ATT_WIDTH // HEAD_DIM
DIL_PATTERNS = ((128, 1), (512, 4), (2048, 16))
SPAN = 128
POOL_WIDTH = 512
POOL_WINDOWS = (2, 4, 8, 16)
POOL_GROUP_DIM = 128
MAX_POOL_WINDOW = 16
IN_WIDTH = 3 * ATT_WIDTH + POOL_WIDTH
D_FF = 2816
CONV_WIDTH = 3
PLE_DIM = 256
EPS = 1e-6

LANES = 128
SUBLANES = 8
HEADS_PER_SLAB = LANES // HEAD_DIM
VMEM_LIMIT_BYTES = 56 * 1024 * 1024

F32 = jnp.float32
BF16 = jnp.bfloat16
U32 = jnp.uint32


def _pack_rows(t):
    return pltpu.bitcast(t.astype(BF16), U32)


def _unpack_rows(t):
    return pltpu.bitcast(t, BF16)


def _rms_scale(t, g):
    inv = lax.rsqrt(jnp.mean(t * t, axis=-1, keepdims=True) + EPS)
    return t * inv * g


QKV_WIDTH = 3 * ATT_WIDTH
LOG2E = 1.4426950408889634
Q_SCALE = (HEAD_DIM ** -0.5) * LOG2E
N_QKV_SLABS = QKV_WIDTH // LANES


def _in_proj_body(x_ref, g_ref, w_ref, pw_ref, ps_ref, r4_ref, r16_ref, pool_ref, zbuf, z4buf,
                  ucarry, *, tm):
    s_idx = pl.program_id(1)
    hw = MAX_POOL_WINDOW

    @pl.when(s_idx == 0)
    def _():
        ucarry[...] = jnp.zeros_like(ucarry)

    hn = _rms_scale(x_ref[...], g_ref[...]).astype(BF16)
    a = ATT_WIDTH

    u = jnp.dot(hn, w_ref[:, 3 * a:], preferred_element_type=F32)
    e = jnp.concatenate([ucarry[...], u], axis=0)
    ucarry[...] = u[tm - hw:, :]
    gd = POOL_GROUP_DIM
    s2 = e + pltpu.roll(e, 1, 0)
    s4 = s2[:, gd:] + pltpu.roll(s2[:, gd:], 2, 0)
    s8 = s4[:, gd:] + pltpu.roll(s4[:, gd:], 4, 0)
    s16 = s8[:, gd:] + pltpu.roll(s8[:, gd:], 8, 0)
    sums = (s2[hw:, 0:gd], s4[hw:, 0:gd], s8[hw:, 0:gd], s16[hw:, :])
    t1 = (s_idx * tm + lax.broadcasted_iota(jnp.int32, (tm, 1), 0) + 1).astype(F32)
    dlts = []
    for g, w in enumerate(POOL_WINDOWS):
        inv_cnt = 1.0 / jnp.minimum(t1, float(w))
        dlts.append((sums[g] * inv_cnt - u[:, g * gd:(g + 1) * gd]).astype(BF16))

    slabs_per_part = a // LANES
    n4 = tm // 4
    for part in range(3):
        z = jnp.dot(hn, w_ref[:, part * a:(part + 1) * a], preferred_element_type=F32)
        if part == 0:
            z = z * Q_SCALE
        for sl in range(slabs_per_part):
            j = part * slabs_per_part + sl
            cols = slice(j * LANES, (j + 1) * LANES)
            zbuf[j] = z[:, sl * LANES:(sl + 1) * LANES]
            for c in range(4):
                cls4 = zbuf[j, pl.ds(c, n4, stride=4), :]
                r4_ref[c, :, cols] = _pack_rows(cls4)
                z4buf[j, c * n4:(c + 1) * n4, :] = cls4
            for c in range(4):
                for b in range(4):
                    cls16 = z4buf[j, pl.ds(c * n4 + b, tm // 16, stride=4), :]
                    r16_ref[4 * b + c, :, cols] = _pack_rows(cls16)

    for g in range(len(POOL_WINDOWS)):
        y = jnp.dot(dlts[g], pw_ref[g], preferred_element_type=F32)
        pool_ref[:, g * gd:(g + 1) * gd] = _pack_rows(y * ps_ref[:, g * gd:(g + 1) * gd])


def _in_proj(x, ln_mix, w_in, pool_w, pool_scale, *, tm):
    B, S, D = x.shape
    grid = (B, S // tm)
    row = lambda b, s: (b, s, 0)
    cls_row = lambda b, s: (b, 0, s, 0)
    const2 = lambda b, s: (0, 0)
    const3 = lambda b, s: (0, 0, 0)
    r4, r16, pool = pl.pallas_call(
        functools.partial(_in_proj_body, tm=tm),
        grid=grid,
        in_specs=[
            pl.BlockSpec((None, tm, D), row),
            pl.BlockSpec((1, D), const2),
            pl.BlockSpec((D, IN_WIDTH), const2),
            pl.BlockSpec((len(POOL_WINDOWS), POOL_GROUP_DIM, POOL_GROUP_DIM), const3),
            pl.BlockSpec((1, POOL_WIDTH), const2),
        ],
        out_specs=[
            pl.BlockSpec((None, 4, tm // 8, QKV_WIDTH), cls_row),
            pl.BlockSpec((None, 16, tm // 32, QKV_WIDTH), cls_row),
            pl.BlockSpec((None, tm // 2, POOL_WIDTH), row),
        ],
        out_shape=[
            jax.ShapeDtypeStruct((B, 4, S // 8, QKV_WIDTH), U32),
            jax.ShapeDtypeStruct((B, 16, S // 32, QKV_WIDTH), U32),
            jax.ShapeDtypeStruct((B, S // 2, POOL_WIDTH), U32),
        ],
        scratch_shapes=[
            pltpu.VMEM((N_QKV_SLABS, tm, LANES), F32),
            pltpu.VMEM((N_QKV_SLABS, tm, LANES), F32),
            pltpu.VMEM((MAX_POOL_WINDOW, POOL_WIDTH), F32),
        ],
        compiler_params=pltpu.CompilerParams(
            dimension_semantics=("arbitrary", "arbitrary"),
            vmem_limit_bytes=VMEM_LIMIT_BYTES),
        name="in_proj_pool",
    )(x, ln_mix.reshape(1, D), w_in.astype(BF16), pool_w.astype(BF16), pool_scale.reshape(1, POOL_WIDTH))
    return r4.reshape(B, S // 2, QKV_WIDTH), r16.reshape(B, S // 2, QKV_WIDTH), pool


N_BIAS_VARIANTS = 3


def _attn_body(slopes_ref, q4_ref, k4_ref, v4_ref, q16_ref, k16_ref, v16_ref, o_ref, bias_s, m_s, d_s, u_s,
               *, seq):
    hp = pl.program_id(1)
    n_blocks = seq // SPAN
    half = SPAN // 2
    lane = lax.broadcasted_iota(jnp.int32, (1, LANES), 1)
    is_h0 = lane < HEAD_DIM
    head_mask = (is_h0, jnp.logical_not(is_h0))
    head_sel = tuple(jnp.where(hm, 1.0, 0.0).astype(BF16) for hm in head_mask)
    ones_cols = tuple(jnp.broadcast_to(sel, (2 * SPAN, LANES)) for sel in head_sel)

    qi = lax.broadcasted_iota(jnp.int32, (SPAN, 2 * SPAN), 0)
    kj = lax.broadcasted_iota(jnp.int32, (SPAN, 2 * SPAN), 1)
    dil_diffs = (qi - kj, qi + SPAN - kj, qi + SPAN - kj)
    dil_extra = (None, kj >= SPAN, None)
    qc, qidx = qi // (SPAN // 4), qi % (SPAN // 4)
    kc, kidx = kj // (SPAN // 2), kj % (SPAN // 2)
    tok_diffs = (4 * (qidx - kidx) + (qc - kc), None, 4 * (qidx - kidx + SPAN // 4) + (qc - kc))
    for p, (_, dil) in enumerate(DIL_PATTERNS):
        for hh in range(HEADS_PER_SLAB):
            slope = slopes_ref[hp * HEADS_PER_SLAB + hh]
            for var in range(N_BIAS_VARIANTS):
                diff = tok_diffs[var] if p == 0 else dil_diffs[var] * dil
                if diff is None:
                    continue
                window = DIL_PATTERNS[p][0]
                valid = (diff >= 0) & (diff <= window)
                if p != 0 and dil_extra[var] is not None:
                    valid = valid & dil_extra[var]
                penalty = (-(slope * LOG2E)) * diff.astype(F32)
                bias_s[p, var, hh] = jnp.where(valid, penalty, -jnp.inf)

    class4_rows = seq // 4 // 2
    branch_refs = ((q4_ref, k4_ref, v4_ref), (q4_ref, k4_ref, v4_ref), (q16_ref, k16_ref, v16_ref))
    for p in (2, 1, 0):
        dil = DIL_PATTERNS[p][1]
        blocks_per_class = n_blocks // dil
        q_ref, k_ref, v_ref = branch_refs[p]

        for g in range(n_blocks):
            cls, jb = divmod(g, blocks_per_class)
            variant = 0 if g == 0 else (1 if jb == 0 else 2)
            row0 = g * half
            if p == 0:
                q0, w0 = g * (half // 4), max(g - 1, 0) * (half // 4)
                load = lambda ref, r0, n: _unpack_rows(jnp.concatenate(
                    [ref[c * class4_rows + r0:c * class4_rows + r0 + n, :] for c in range(4)], axis=0))
                qb = load(q_ref, q0, half // 4)
                kw = load(k_ref, w0, half // 2)
                vw = load(v_ref, w0, half // 2)
                state_rows = [pl.ds(g * SPAN + c, SPAN // 4, stride=4) for c in range(4)]
            else:
                win0 = max(g - 1, 0) * half
                qb = _unpack_rows(q_ref[row0:row0 + half, :])
                kw = _unpack_rows(k_ref[win0:win0 + 2 * half, :])
                vw = _unpack_rows(v_ref[win0:win0 + 2 * half, :])
                state_rows = [pl.ds((dil * SPAN) * jb + cls, SPAN, stride=dil)]
            ms, es, vs = [], [], []
            for hh in range(HEADS_PER_SLAB):
                qh = qb * head_sel[hh]
                s = lax.dot_general(qh, kw, (((1,), (1,)), ((), ())), preferred_element_type=F32)
                s = s + bias_s[p, variant, hh]
                m = jnp.max(s, axis=1, keepdims=True)
                ms.append(m)
                es.append(jnp.exp2(s - m).astype(BF16))
                vs.append(jnp.concatenate(
                    [jnp.where(head_mask[hh], vw, jnp.zeros_like(vw)), ones_cols[hh]], axis=1))
            ud = jnp.dot(jnp.concatenate(es, axis=1), jnp.concatenate(vs, axis=0),
                         preferred_element_type=F32)
            m_new = jnp.where(is_h0, ms[0], ms[1])
            u_new = ud[:, 0:LANES]
            d_new = ud[:, LANES:]

            def read(ref):
                return jnp.concatenate([ref[rows, :] for rows in state_rows], axis=0)

            def write(ref, val):
                n = SPAN // len(state_rows)
                for i, rows in enumerate(state_rows):
                    ref[rows, :] = val[i * n:(i + 1) * n, :]

            if p == 2:
                write(m_s, m_new)
                write(d_s, d_new)
                write(u_s, u_new)
            else:
                m_old = read(m_s)
                m_all = jnp.maximum(m_old, m_new)
                w_old = jnp.exp2(m_old - m_all)
                w_new = jnp.exp2(m_new - m_all)
                d_all = read(d_s) * w_old + d_new * w_new
                u_all = read(u_s) * w_old + u_new * w_new
                if p == 1:
                    write(m_s, m_all)
                    write(d_s, d_all)
                    write(u_s, u_all)
                else:
                    write(u_s, u_all / d_all)
                    o_ref[row0:row0 + half, :] = _pack_rows(u_s[g * SPAN:(g + 1) * SPAN, :])


def _attention(qkv_r4, qkv_r16, slopes):
    B, half_s, _ = qkv_r4.shape
    S = 2 * half_s
    n_slabs = ATT_WIDTH // LANES

    def slab(part):
        return pl.BlockSpec((None, half_s, LANES), lambda b, h, slopes_ref: (b, 0, part * n_slabs + h))

    qkv_specs = [slab(0), slab(1), slab(2)]
    n_br = len(DIL_PATTERNS)
    return pl.pallas_call(
        functools.partial(_attn_body, seq=S),
        grid_spec=pltpu.PrefetchScalarGridSpec(
            num_scalar_prefetch=1,
            grid=(B, n_slabs),
            in_specs=qkv_specs * 2,
            out_specs=pl.BlockSpec((None, half_s, LANES), lambda b, h, slopes_ref: (b, 0, h)),
            scratch_shapes=[
                pltpu.VMEM((n_br, N_BIAS_VARIANTS, HEADS_PER_SLAB, SPAN, 2 * SPAN), F32),
                pltpu.VMEM((S, LANES), F32),
                pltpu.VMEM((S, LANES), F32),
                pltpu.VMEM((S, LANES), F32),
            ],
        ),
        out_shape=jax.ShapeDtypeStruct((B, half_s, ATT_WIDTH), U32),
        compiler_params=pltpu.CompilerParams(
            dimension_semantics=("arbitrary", "arbitrary"),
            vmem_limit_bytes=VMEM_LIMIT_BYTES),
        name="dilated_attention",
    )(slopes, qkv_r4, qkv_r4, qkv_r4, qkv_r16, qkv_r16, qkv_r16)


FF_CHUNK = 256
N_STAGE_SLOTS = 4


def _ffn_body(x_ref, att_ref, pool_ref, p_ref, wout_ref, lnf_ref, wup_ref, cw_ref, cb_ref, wdn_ref,
              lnp_ref, wg_ref, wple_ref, lnfin_ref, o_ref, act_s, carry_s, stage_s, *, tm):
    s_idx = pl.program_id(1)

    @pl.when(s_idx == 0)
    def _():
        carry_s[...] = jnp.zeros_like(carry_s)

    h = x_ref[...]
    h = h + jnp.dot(_unpack_rows(att_ref[...]), wout_ref[0:ATT_WIDTH, :], preferred_element_type=F32)
    h = h + jnp.dot(_unpack_rows(pool_ref[...]), wout_ref[ATT_WIDTH:, :], preferred_element_type=F32)

    hn = _rms_scale(h, lnf_ref[...]).astype(BF16)

    def conv(c0, slot):
        cols = slice(c0, c0 + FF_CHUNK)
        pre = jnp.dot(hn, wup_ref[:, cols], preferred_element_type=F32)
        sh1, sh2 = [], []
        for sl in range(FF_CHUNK // LANES):
            lanes = slice(sl * LANES, (sl + 1) * LANES)
            stage_s[slot, sl, pl.ds(0, SUBLANES, stride=2), :] = carry_s[:, c0 + sl * LANES:c0 + (sl + 1) * LANES]
            stage_s[slot, sl, pl.ds(2 * SUBLANES, tm, stride=2), :] = pre[:, lanes]
            sh1.append(stage_s[slot, sl, pl.ds(2 * SUBLANES - 2, tm, stride=2), :])
            sh2.append(stage_s[slot, sl, pl.ds(2 * SUBLANES - 4, tm, stride=2), :])
        carry_s[:, cols] = pre[tm - SUBLANES:, :]
        y = cb_ref[:, cols] + cw_ref[0:1, cols] * jnp.concatenate(sh2, axis=1)
        y = y + cw_ref[1:2, cols] * jnp.concatenate(sh1, axis=1)
        return y + cw_ref[2:3, cols] * pre

    for c in range(D_FF // FF_CHUNK):
        c0 = c * FF_CHUNK
        gate = conv(c0, (2 * c) % N_STAGE_SLOTS)
        val = conv(D_FF + c0, (2 * c + 1) % N_STAGE_SLOTS)
        act_s[:, c0:c0 + FF_CHUNK] = (gate * jax.nn.sigmoid(gate) * val).astype(BF16)

    h = h + jnp.dot(act_s[...], wdn_ref[...], preferred_element_type=F32)

    hn = _rms_scale(h, lnp_ref[...]).astype(BF16)
    g = jax.nn.sigmoid(jnp.dot(hn, wg_ref[...], preferred_element_type=F32))
    emb = jnp.dot(p_ref[...].astype(BF16), wple_ref[...], preferred_element_type=F32)
    h = h + g * emb
    o_ref[...] = _rms_scale(h, lnfin_ref[...])


def _ffn(x, att, pool, p, w_out, ln_ffn, w_up, conv_w, conv_b, w_down, ln_ple, w_ple_gate, w_ple,
         ln_final, *, tm):
    B, S, D = x.shape
    grid = (B, S // tm)
    row = lambda b, s: (b, s, 0)
    const2 = lambda b, s: (0, 0)

    def resident(shape):
        return pl.BlockSpec(shape, const2, pipeline_mode=pl.Buffered(1))

    return pl.pallas_call(
        functools.partial(_ffn_body, tm=tm),
        grid=grid,
        in_specs=[
            pl.BlockSpec((None, tm, D), row),
            pl.BlockSpec((None, tm // 2, ATT_WIDTH), row),
            pl.BlockSpec((None, tm // 2, POOL_WIDTH), row),
            pl.BlockSpec((None, tm, PLE_DIM), row),
            resident((ATT_WIDTH + POOL_WIDTH, D)),
            resident((1, D)),
            resident((D, 2 * D_FF)),
            resident((CONV_WIDTH, 2 * D_FF)),
            resident((1, 2 * D_FF)),
            resident((D_FF, D)),
            resident((1, D)),
            resident((D, D)),
            resident((PLE_DIM, D)),
            resident((1, D)),
        ],
        out_specs=pl.BlockSpec((None, tm, D), row),
        out_shape=jax.ShapeDtypeStruct((B, S, D), F32),
        scratch_shapes=[
            pltpu.VMEM((tm, D_FF), BF16),
            pltpu.VMEM((SUBLANES, 2 * D_FF), F32),
            pltpu.VMEM((N_STAGE_SLOTS, FF_CHUNK // LANES, 2 * (SUBLANES + tm), LANES), F32),
        ],
        compiler_params=pltpu.CompilerParams(
            dimension_semantics=("arbitrary", "arbitrary"),
            vmem_limit_bytes=VMEM_LIMIT_BYTES),
        name="outproj_ffn_ple",
    )(x, att, pool, p, w_out.astype(BF16), ln_ffn.reshape(1, D), w_up.astype(BF16), conv_w,
      conv_b.reshape(1, 2 * D_FF), w_down.astype(BF16), ln_ple.reshape(1, D), w_ple_gate.astype(BF16),
      w_ple.astype(BF16), ln_final.reshape(1, D))


def kernel(x, p, ln_mix, w_in, pool_w, pool_scale, w_out, ln_ffn, w_up, conv_w, conv_b, w_down, ln_ple,
           w_ple_gate, w_ple, ln_final):
    depth = p.shape[0]
    slopes = jnp.exp2(-8.0 * (jnp.arange(N_ATT_HEADS, dtype=F32) + 1.0) / N_ATT_HEADS)
    assert depth == 1, "the final RMSNorm is fused into the last layer's kernel"
    i = 0
    qkv_r4, qkv_r16, pool = _in_proj(x, ln_mix[i], w_in[i], pool_w[i], pool_scale[i], tm=512)
    att = _attention(qkv_r4, qkv_r16, slopes)
    return _ffn(x, att, pool, p[i], w_out[i], ln_ffn[i], w_up[i], conv_w[i], conv_b[i], w_down[i],
                ln_ple[i], w_ple_gate[i], w_ple[i], ln_final, tm=512)
```

```python
import functools

import jax
import jax.numpy as jnp
from jax import lax
from jax.experimental import pallas as pl
from jax.experimental.pallas import tpu as pltpu

D_MODEL = 1024
HEAD_DIM = 64
ATT_WIDTH = 512
N_ATT_HEADS = ATT_WIDTH // HEAD_DIM
DIL_PATTERNS = ((128, 1), (512, 4), (2048, 16))
SPAN = 128
POOL_WIDTH = 512
POOL_WINDOWS = (2, 4, 8, 16)
POOL_GROUP_DIM = 128
MAX_POOL_WINDOW = 16
IN_WIDTH = 3 * ATT_WIDTH + POOL_WIDTH
D_FF = 2816
CONV_WIDTH = 3
PLE_DIM = 256
EPS = 1e-6

LANES = 128
SUBLANES = 8
HEADS_PER_SLAB = LANES // HEAD_DIM
VMEM_LIMIT_BYTES = 56 * 1024 * 1024

F32 = jnp.float32
BF16 = jnp.bfloat16
U32 = jnp.uint32


def _pack_rows(t):
    return pltpu.bitcast(t.astype(BF16), U32)


def _unpack_rows(t):
    return pltpu.bitcast(t, BF16)


def _rms_scale(t, g):
    inv = lax.rsqrt(jnp.mean(t * t, axis=-1, keepdims=True) + EPS)
    return t * inv * g


QKV_WIDTH = 3 * ATT_WIDTH
LOG2E = 1.4426950408889634
Q_SCALE = (HEAD_DIM ** -0.5) * LOG2E
N_QKV_SLABS = QKV_WIDTH // LANES


def _in_proj_body(x_ref, g_ref, w_ref, pw_ref, ps_ref, nat_ref, r4_ref, r16_ref, pool_ref, zbuf, z4buf,
                  ucarry, *, tm):
    s_idx = pl.program_id(1)
    hw = MAX_POOL_WINDOW

    @pl.when(s_idx == 0)
    def _():
        ucarry[...] = jnp.zeros_like(ucarry)

    hn = _rms_scale(x_ref[...], g_ref[...]).astype(BF16)
    a = ATT_WIDTH

    u = jnp.dot(hn, w_ref[:, 3 * a:], preferred_element_type=F32)
    e = jnp.concatenate([ucarry[...], u], axis=0)
    ucarry[...] = u[tm - hw:, :]
    gd = POOL_GROUP_DIM
    s2 = e + pltpu.roll(e, 1, 0)
    s4 = s2[:, gd:] + pltpu.roll(s2[:, gd:], 2, 0)
    s8 = s4[:, gd:] + pltpu.roll(s4[:, gd:], 4, 0)
    s16 = s8[:, gd:] + pltpu.roll(s8[:, gd:], 8, 0)
    sums = (s2[hw:, 0:gd], s4[hw:, 0:gd], s8[hw:, 0:gd], s16[hw:, :])
    t1 = (s_idx * tm + lax.broadcasted_iota(jnp.int32, (tm, 1), 0) + 1).astype(F32)
    dlts = []
    for g, w in enumerate(POOL_WINDOWS):
        inv_cnt = 1.0 / jnp.minimum(t1, float(w))
        dlts.append((sums[g] * inv_cnt - u[:, g * gd:(g + 1) * gd]).astype(BF16))

    slabs_per_part = a // LANES
    n4 = tm // 4
    for part in range(3):
        z = jnp.dot(hn, w_ref[:, part * a:(part + 1) * a], preferred_element_type=F32)
        if part == 0:
            z = z * Q_SCALE
        nat_ref[:, part * a:(part + 1) * a] = _pack_rows(z)
        for sl in range(slabs_per_part):
            j = part * slabs_per_part + sl
            cols = slice(j * LANES, (j + 1) * LANES)
            zbuf[j] = z[:, sl * LANES:(sl + 1) * LANES]
            for c in range(4):
                cls4 = zbuf[j, pl.ds(c, n4, stride=4), :]
                r4_ref[c, :, cols] = _pack_rows(cls4)
                z4buf[j, c * n4:(c + 1) * n4, :] = cls4
            for c in range(4):
                for b in range(4):
                    cls16 = z4buf[j, pl.ds(c * n4 + b, tm // 16, stride=4), :]
                    r16_ref[4 * b + c, :, cols] = _pack_rows(cls16)

    for g in range(len(POOL_WINDOWS)):
        y = jnp.dot(dlts[g], pw_ref[g], preferred_element_type=F32)
        pool_ref[:, g * gd:(g + 1) * gd] = _pack_rows(y * ps_ref[:, g * gd:(g + 1) * gd])


def _in_proj(x, ln_mix, w_in, pool_w, pool_scale, *, tm):
    B, S, D = x.shape
    grid = (B, S // tm)
    row = lambda b, s: (b, s, 0)
    cls_row = lambda b, s: (b, 0, s, 0)
    const2 = lambda b, s: (0, 0)
    const3 = lambda b, s: (0, 0, 0)
    nat, r4, r16, pool = pl.pallas_call(
        functools.partial(_in_proj_body, tm=tm),
        grid=grid,
        in_specs=[
            pl.BlockSpec((None, tm, D), row),
            pl.BlockSpec((1, D), const2),
            pl.BlockSpec((D, IN_WIDTH), const2),
            pl.BlockSpec((len(POOL_WINDOWS), POOL_GROUP_DIM, POOL_GROUP_DIM), const3),
            pl.BlockSpec((1, POOL_WIDTH), const2),
        ],
        out_specs=[
            pl.BlockSpec((None, tm // 2, QKV_WIDTH), row),
            pl.BlockSpec((None, 4, tm // 8, QKV_WIDTH), cls_row),
            pl.BlockSpec((None, 16, tm // 32, QKV_WIDTH), cls_row),
            pl.BlockSpec((None, tm // 2, POOL_WIDTH), row),
        ],
        out_shape=[
            jax.ShapeDtypeStruct((B, S // 2, QKV_WIDTH), U32),
            jax.ShapeDtypeStruct((B, 4, S // 8, QKV_WIDTH), U32),
            jax.ShapeDtypeStruct((B, 16, S // 32, QKV_WIDTH), U32),
            jax.ShapeDtypeStruct((B, S // 2, POOL_WIDTH), U32),
        ],
        scratch_shapes=[
            pltpu.VMEM((N_QKV_SLABS, tm, LANES), F32),
            pltpu.VMEM((N_QKV_SLABS, tm, LANES), F32),
            pltpu.VMEM((MAX_POOL_WINDOW, POOL_WIDTH), F32),
        ],
        compiler_params=pltpu.CompilerParams(
            dimension_semantics=("arbitrary", "arbitrary"),
            vmem_limit_bytes=VMEM_LIMIT_BYTES),
        name="in_proj_pool",
    )(x, ln_mix.reshape(1, D), w_in.astype(BF16), pool_w.astype(BF16), pool_scale.reshape(1, POOL_WIDTH))
    return nat, r4.reshape(B, S // 2, QKV_WIDTH), r16.reshape(B, S // 2, QKV_WIDTH), pool


N_BIAS_VARIANTS = 3


def _attn_body(slopes_ref, qn_ref, kn_ref, vn_ref, q4_ref, k4_ref, v4_ref, q16_ref, k16_ref, v16_ref,
               o_ref, bias_s, m_s, d_s, u_s, *, seq):
    hp = pl.program_id(1)
    n_blocks = seq // SPAN
    half = SPAN // 2
    lane = lax.broadcasted_iota(jnp.int32, (1, LANES), 1)
    is_h0 = lane < HEAD_DIM
    head_mask = (is_h0, jnp.logical_not(is_h0))
    head_sel = tuple(jnp.where(hm, 1.0, 0.0).astype(BF16) for hm in head_mask)
    ones_cols = tuple(jnp.broadcast_to(sel, (2 * SPAN, LANES)) for sel in head_sel)

    qi = lax.broadcasted_iota(jnp.int32, (SPAN, 2 * SPAN), 0)
    kj = lax.broadcasted_iota(jnp.int32, (SPAN, 2 * SPAN), 1)
    diffs = (qi - kj, qi + SPAN - kj, qi + SPAN - kj)
    extra = (None, kj >= SPAN, None)
    for p, (_, dil) in enumerate(DIL_PATTERNS):
        for hh in range(HEADS_PER_SLAB):
            slope = slopes_ref[hp * HEADS_PER_SLAB + hh]
            for var in range(N_BIAS_VARIANTS):
                diff = diffs[var]
                valid = (diff >= 0) & (diff <= SPAN)
                if extra[var] is not None:
                    valid = valid & extra[var]
                penalty = (-(slope * (float(dil) * LOG2E))) * diff.astype(F32)
                bias_s[p, var, hh] = jnp.where(valid, penalty, -jnp.inf)

    branch_refs = ((qn_ref, kn_ref, vn_ref), (q4_ref, k4_ref, v4_ref), (q16_ref, k16_ref, v16_ref))
    for p in (2, 1, 0):
        dil = DIL_PATTERNS[p][1]
        blocks_per_class = n_blocks // dil
        q_ref, k_ref, v_ref = branch_refs[p]

        for g in range(n_blocks):
            cls, jb = divmod(g, blocks_per_class)
            variant = 0 if g == 0 else (1 if jb == 0 else 2)
            row0 = g * half
            win0 = max(g - 1, 0) * half
            qb = _unpack_rows(q_ref[row0:row0 + half, :])
            kw = _unpack_rows(k_ref[win0:win0 + 2 * half, :])
            vw = _unpack_rows(v_ref[win0:win0 + 2 * half, :])
            ms, es, vs = [], [], []
            for hh in range(HEADS_PER_SLAB):
                qh = qb * head_sel[hh]
                s = lax.dot_general(qh, kw, (((1,), (1,)), ((), ())), preferred_element_type=F32)
                s = s + bias_s[p, variant, hh]
                m = jnp.max(s, axis=1, keepdims=True)
                ms.append(m)
                es.append(jnp.exp2(s - m).astype(BF16))
                vs.append(jnp.concatenate(
                    [jnp.where(head_mask[hh], vw, jnp.zeros_like(vw)), ones_cols[hh]], axis=1))
            ud = jnp.dot(jnp.concatenate(es, axis=1), jnp.concatenate(vs, axis=0),
                         preferred_element_type=F32)
            m_new = jnp.where(is_h0, ms[0], ms[1])
            u_new = ud[:, 0:LANES]
            d_new = ud[:, LANES:]
            start = (dil * SPAN) * jb + cls
            rows = pl.ds(start, SPAN) if dil == 1 else pl.ds(start, SPAN, stride=dil)
            if p == 2:
                m_s[rows, :] = m_new
                d_s[rows, :] = d_new
                u_s[rows, :] = u_new
            else:
                m_old = m_s[rows, :]
                m_all = jnp.maximum(m_old, m_new)
                w_old = jnp.exp2(m_old - m_all)
                w_new = jnp.exp2(m_new - m_all)
                d_all = d_s[rows, :] * w_old + d_new * w_new
                u_all = u_s[rows, :] * w_old + u_new * w_new
                if p == 1:
                    m_s[rows, :] = m_all
                    d_s[rows, :] = d_all
                    u_s[rows, :] = u_all
                else:
                    o_ref[row0:row0 + half, :] = _pack_rows(u_all / d_all)


def _attention(qkv_nat, qkv_r4, qkv_r16, slopes):
    B, half_s, _ = qkv_nat.shape
    S = 2 * half_s
    n_slabs = ATT_WIDTH // LANES

    def slab(part):
        return pl.BlockSpec((None, half_s, LANES), lambda b, h, slopes_ref: (b, 0, part * n_slabs + h))

    qkv_specs = [slab(0), slab(1), slab(2)]
    n_br = len(DIL_PATTERNS)
    return pl.pallas_call(
        functools.partial(_attn_body, seq=S),
        grid_spec=pltpu.PrefetchScalarGridSpec(
            num_scalar_prefetch=1,
            grid=(B, n_slabs),
            in_specs=qkv_specs * n_br,
            out_specs=pl.BlockSpec((None, half_s, LANES), lambda b, h, slopes_ref: (b, 0, h)),
            scratch_shapes=[
                pltpu.VMEM((n_br, N_BIAS_VARIANTS, HEADS_PER_SLAB, SPAN, 2 * SPAN), F32),
                pltpu.VMEM((S, LANES), F32),
                pltpu.VMEM((S, LANES), F32),
                pltpu.VMEM((S, LANES), F32),
            ],
        ),
        out_shape=jax.ShapeDtypeStruct((B, half_s, ATT_WIDTH), U32),
        compiler_params=pltpu.CompilerParams(
            dimension_semantics=("arbitrary", "arbitrary"),
            vmem_limit_bytes=VMEM_LIMIT_BYTES),
        name="dilated_attention",
    )(slopes, qkv_nat, qkv_nat, qkv_nat, qkv_r4, qkv_r4, qkv_r4, qkv_r16, qkv_r16, qkv_r16)


FF_CHUNK = 256
N_STAGE_SLOTS = 4


def _ffn_body(x_ref, att_ref, pool_ref, p_ref, wout_ref, lnf_ref, wup_ref, cw_ref, cb_ref, wdn_ref,
              lnp_ref, wg_ref, wple_ref, lnfin_ref, o_ref, act_s, carry_s, stage_s, *, tm):
    s_idx = pl.program_id(1)

    @pl.when(s_idx == 0)
    def _():
        carry_s[...] = jnp.zeros_like(carry_s)

    th = tm // 2
    halves = (slice(0, th), slice(th, tm))
    packed = (slice(0, th // 2), slice(th // 2, tm // 2))

    hs = []
    for k in range(2):
        hk = x_ref[halves[k], :]
        hk = hk + jnp.dot(_unpack_rows(att_ref[packed[k], :]), wout_ref[0:ATT_WIDTH, :],
                          preferred_element_type=F32)
        hk = hk + jnp.dot(_unpack_rows(pool_ref[packed[k], :]), wout_ref[ATT_WIDTH:, :],
                          preferred_element_type=F32)
        hs.append(hk)
    hns = [_rms_scale(hk, lnf_ref[...]).astype(BF16) for hk in hs]

    def conv(hn, c0, slot):
        cols = slice(c0, c0 + FF_CHUNK)
        pre = jnp.dot(hn, wup_ref[:, cols], preferred_element_type=F32)
        sh1, sh2 = [], []
        for sl in range(FF_CHUNK // LANES):
            lanes = slice(sl * LANES, (sl + 1) * LANES)
            stage_s[slot, sl, pl.ds(0, SUBLANES, stride=2), :] = carry_s[:, c0 + sl * LANES:c0 + (sl + 1) * LANES]
            stage_s[slot, sl, pl.ds(2 * SUBLANES, th, stride=2), :] = pre[:, lanes]
            sh1.append(stage_s[slot, sl, pl.ds(2 * SUBLANES - 2, th, stride=2), :])
            sh2.append(stage_s[slot, sl, pl.ds(2 * SUBLANES - 4, th, stride=2), :])
        carry_s[:, cols] = pre[th - SUBLANES:, :]
        y = cb_ref[:, cols] + cw_ref[0:1, cols] * jnp.concatenate(sh2, axis=1)
        y = y + cw_ref[1:2, cols] * jnp.concatenate(sh1, axis=1)
        return y + cw_ref[2:3, cols] * pre

    n_chunks = D_FF // FF_CHUNK
    for k in range(2):
        for c in range(n_chunks):
            c0 = c * FF_CHUNK
            slot = 2 * (k * n_chunks + c)
            gate = conv(hns[k], c0, slot % N_STAGE_SLOTS)
            val = conv(hns[k], D_FF + c0, (slot + 1) % N_STAGE_SLOTS)
            act_s[halves[k], c0:c0 + FF_CHUNK] = (gate * jax.nn.sigmoid(gate) * val).astype(BF16)

    hs = [hs[k] + jnp.dot(act_s[halves[k], :], wdn_ref[...], preferred_element_type=F32) for k in range(2)]
    for k in range(2):
        hn = _rms_scale(hs[k], lnp_ref[...]).astype(BF16)
        g = jax.nn.sigmoid(jnp.dot(hn, wg_ref[...], preferred_element_type=F32))
        emb = jnp.dot(p_ref[halves[k], :].astype(BF16), wple_ref[...], preferred_element_type=F32)
        o_ref[halves[k], :] = _rms_scale(hs[k] + g * emb, lnfin_ref[...])


def _ffn(x, att, pool, p, w_out, ln_ffn, w_up, conv_w, conv_b, w_down, ln_ple, w_ple_gate, w_ple,
         ln_final, *, tm):
    B, S, D = x.shape
    grid = (B, S // tm)
    row = lambda b, s: (b, s, 0)
    const2 = lambda b, s: (0, 0)

    def resident(shape):
        return pl.BlockSpec(shape, const2, pipeline_mode=pl.Buffered(1))

    return pl.pallas_call(
        functools.partial(_ffn_body, tm=tm),
        grid=grid,
        in_specs=[
            pl.BlockSpec((None, tm, D), row),
            pl.BlockSpec((None, tm // 2, ATT_WIDTH), row),
            pl.BlockSpec((None, tm // 2, POOL_WIDTH), row),
            pl.BlockSpec((None, tm, PLE_DIM), row),
            resident((ATT_WIDTH + POOL_WIDTH, D)),
            resident((1, D)),
            resident((D, 2 * D_FF)),
            resident((CONV_WIDTH, 2 * D_FF)),
            resident((1, 2 * D_FF)),
            resident((D_FF, D)),
            resident((1, D)),
            resident((D, D)),
            resident((PLE_DIM, D)),
            resident((1, D)),
        ],
        out_specs=pl.BlockSpec((None, tm, D), row),
        out_shape=jax.ShapeDtypeStruct((B, S, D), F32),
        scratch_shapes=[
            pltpu.VMEM((tm, D_FF), BF16),
            pltpu.VMEM((SUBLANES, 2 * D_FF), F32),
            pltpu.VMEM((N_STAGE_SLOTS, FF_CHUNK // LANES, 2 * (SUBLANES + tm // 2), LANES), F32),
        ],
        compiler_params=pltpu.CompilerParams(
            dimension_semantics=("arbitrary", "arbitrary"),
            vmem_limit_bytes=VMEM_LIMIT_BYTES),
        name="outproj_ffn_ple",
    )(x, att, pool, p, w_out.astype(BF16), ln_ffn.reshape(1, D), w_up.astype(BF16), conv_w,
      conv_b.reshape(1, 2 * D_FF), w_down.astype(BF16), ln_ple.reshape(1, D), w_ple_gate.astype(BF16),
      w_ple.astype(BF16), ln_final.reshape(1, D))


def kernel(x, p, ln_mix, w_in, pool_w, pool_scale, w_out, ln_ffn, w_up, conv_w, conv_b, w_down, ln_ple,
           w_ple_gate, w_ple, ln_final):
    depth = p.shape[0]
    slopes = jnp.exp2(-8.0 * (jnp.arange(N_ATT_HEADS, dtype=F32) + 1.0) / N_ATT_HEADS)
    assert depth == 1, "the final RMSNorm is fused into the last layer's kernel"
    i = 0
    qkv_nat, qkv_r4, qkv_r16, pool = _in_proj(x, ln_mix[i], w_in[i], pool_w[i], pool_scale[i], tm=512)
    att = _attention(qkv_nat, qkv_r4, qkv_r16, slopes)
    return _ffn(x, att, pool, p[i], w_out[i], ln_ffn[i], w_up[i], conv_w[i], conv_b[i], w_down[i],
                ln_ple[i], w_ple_gate[i], w_ple[i], ln_final, tm=512)
```

```python
import functools

import jax
import jax.numpy as jnp
from jax import lax
from jax.experimental import pallas as pl
from jax.experimental.pallas import tpu as pltpu

D_MODEL = 1024
HEAD_DIM = 64
ATT_WIDTH = 512
N_ATT_HEADS = ATT_WIDTH // HEAD_DIM
DIL_PATTERNS = ((128, 1), (512, 4), (2048, 16))
SPAN = 128
POOL_WIDTH = 512
POOL_WINDOWS = (2, 4, 8, 16)
POOL_GROUP_DIM = 128
MAX_POOL_WINDOW = 16
IN_WIDTH = 3 * ATT_WIDTH + POOL_WIDTH
D_FF = 2816
CONV_WIDTH = 3
PLE_DIM = 256
EPS = 1e-6

LANES = 128
SUBLANES = 8
HEADS_PER_SLAB = LANES // HEAD_DIM
VMEM_LIMIT_BYTES = 56 * 1024 * 1024

F32 = jnp.float32
BF16 = jnp.bfloat16
U32 = jnp.uint32


def _pack_rows(t):
    return pltpu.bitcast(t.astype(BF16), U32)


def _unpack_rows(t):
    return pltpu.bitcast(t, BF16)


def _rms_scale(t, g):
    inv = lax.rsqrt(jnp.mean(t * t, axis=-1, keepdims=True) + EPS)
    return t * inv * g


QKV_WIDTH = 3 * ATT_WIDTH
LOG2E = 1.4426950408889634
Q_SCALE = (HEAD_DIM ** -0.5) * LOG2E
N_QKV_SLABS = QKV_WIDTH // LANES


def _in_proj_body(x_ref, g_ref, w_ref, pw_ref, ps_ref, *rest, tm, n_cast):
    cast_in, rest = rest[:n_cast], rest[n_cast:]
    (nat_ref, r4_ref, r16_ref, pool_ref), rest = rest[:4], rest[4:]
    cast_out, (zbuf, z4buf, ucarry) = rest[:n_cast], rest[n_cast:]
    s_idx = pl.program_id(1)
    hw = MAX_POOL_WINDOW

    @pl.when(s_idx == 0)
    def _():
        ucarry[...] = jnp.zeros_like(ucarry)

    for src, dst in zip(cast_in, cast_out):
        dst[...] = src[...].astype(BF16)

    hn = _rms_scale(x_ref[...], g_ref[...]).astype(BF16)
    a = ATT_WIDTH

    u = jnp.dot(hn, w_ref[:, 3 * a:], preferred_element_type=F32)
    e = jnp.concatenate([ucarry[...], u], axis=0)
    ucarry[...] = u[tm - hw:, :]
    gd = POOL_GROUP_DIM
    s2 = e + pltpu.roll(e, 1, 0)
    s4 = s2[:, gd:] + pltpu.roll(s2[:, gd:], 2, 0)
    s8 = s4[:, gd:] + pltpu.roll(s4[:, gd:], 4, 0)
    s16 = s8[:, gd:] + pltpu.roll(s8[:, gd:], 8, 0)
    sums = (s2[hw:, 0:gd], s4[hw:, 0:gd], s8[hw:, 0:gd], s16[hw:, :])
    t1 = (s_idx * tm + lax.broadcasted_iota(jnp.int32, (tm, 1), 0) + 1).astype(F32)
    dlts = []
    for g, w in enumerate(POOL_WINDOWS):
        inv_cnt = 1.0 / jnp.minimum(t1, float(w))
        dlts.append((sums[g] * inv_cnt - u[:, g * gd:(g + 1) * gd]).astype(BF16))

    slabs_per_part = a // LANES
    n4 = tm // 4
    for part in range(3):
        z = jnp.dot(hn, w_ref[:, part * a:(part + 1) * a], preferred_element_type=F32)
        if part == 0:
            z = z * Q_SCALE
        nat_ref[:, part * a:(part + 1) * a] = _pack_rows(z)
        for sl in range(slabs_per_part):
            j = part * slabs_per_part + sl
            cols = slice(j * LANES, (j + 1) * LANES)
            zbuf[j] = z[:, sl * LANES:(sl + 1) * LANES]
            for c in range(4):
                cls4 = zbuf[j, pl.ds(c, n4, stride=4), :]
                r4_ref[c, :, cols] = _pack_rows(cls4)
                z4buf[j, c * n4:(c + 1) * n4, :] = cls4
            for c in range(4):
                for b in range(4):
                    cls16 = z4buf[j, pl.ds(c * n4 + b, tm // 16, stride=4), :]
                    r16_ref[4 * b + c, :, cols] = _pack_rows(cls16)

    for g in range(len(POOL_WINDOWS)):
        y = jnp.dot(dlts[g], pw_ref[g], preferred_element_type=F32)
        pool_ref[:, g * gd:(g + 1) * gd] = _pack_rows(y * ps_ref[:, g * gd:(g + 1) * gd])


def _cast_block_rows(rows, n_steps):
    br = 2 * SUBLANES
    while rows % br or rows // br > n_steps:
        br += 2 * SUBLANES
    return br


def _in_proj(x, ln_mix, w_in, pool_w, pool_scale, later_weights, *, tm):
    B, S, D = x.shape
    grid = (B, S // tm)
    n_steps = B * (S // tm)
    cast_specs = []
    for w in later_weights:
        br = _cast_block_rows(w.shape[0], n_steps)
        last = w.shape[0] // br - 1
        cast_specs.append(pl.BlockSpec(
            (br, w.shape[1]), lambda b, s, last=last: (jnp.minimum(b * (S // tm) + s, last), 0)))
    row = lambda b, s: (b, s, 0)
    cls_row = lambda b, s: (b, 0, s, 0)
    const2 = lambda b, s: (0, 0)
    const3 = lambda b, s: (0, 0, 0)
    nat, r4, r16, pool, *casted = pl.pallas_call(
        functools.partial(_in_proj_body, tm=tm, n_cast=len(later_weights)),
        grid=grid,
        in_specs=[
            pl.BlockSpec((None, tm, D), row),
            pl.BlockSpec((1, D), const2),
            pl.BlockSpec((D, IN_WIDTH), const2),
            pl.BlockSpec((len(POOL_WINDOWS), POOL_GROUP_DIM, POOL_GROUP_DIM), const3),
            pl.BlockSpec((1, POOL_WIDTH), const2),
        ] + cast_specs,
        out_specs=[
            pl.BlockSpec((None, tm // 2, QKV_WIDTH), row),
            pl.BlockSpec((None, 4, tm // 8, QKV_WIDTH), cls_row),
            pl.BlockSpec((None, 16, tm // 32, QKV_WIDTH), cls_row),
            pl.BlockSpec((None, tm // 2, POOL_WIDTH), row),
        ] + cast_specs,
        out_shape=[
            jax.ShapeDtypeStruct((B, S // 2, QKV_WIDTH), U32),
            jax.ShapeDtypeStruct((B, 4, S // 8, QKV_WIDTH), U32),
            jax.ShapeDtypeStruct((B, 16, S // 32, QKV_WIDTH), U32),
            jax.ShapeDtypeStruct((B, S // 2, POOL_WIDTH), U32),
        ] + [jax.ShapeDtypeStruct(w.shape, BF16) for w in later_weights],
        scratch_shapes=[
            pltpu.VMEM((N_QKV_SLABS, tm, LANES), F32),
            pltpu.VMEM((N_QKV_SLABS, tm, LANES), F32),
            pltpu.VMEM((MAX_POOL_WINDOW, POOL_WIDTH), F32),
        ],
        compiler_params=pltpu.CompilerParams(
            dimension_semantics=("arbitrary", "arbitrary"),
            vmem_limit_bytes=VMEM_LIMIT_BYTES),
        name="in_proj_pool",
    )(x, ln_mix.reshape(1, D), w_in.astype(BF16), pool_w.astype(BF16), pool_scale.reshape(1, POOL_WIDTH),
      *later_weights)
    return nat, r4.reshape(B, S // 2, QKV_WIDTH), r16.reshape(B, S // 2, QKV_WIDTH), pool, casted


N_BIAS_VARIANTS = 3


def _attn_body(slopes_ref, qn_ref, kn_ref, vn_ref, q4_ref, k4_ref, v4_ref, q16_ref, k16_ref, v16_ref,
               o_ref, bias_s, m_s, d_s, u_s, *, seq):
    hp = pl.program_id(1)
    n_blocks = seq // SPAN
    half = SPAN // 2
    lane = lax.broadcasted_iota(jnp.int32, (1, LANES), 1)
    is_h0 = lane < HEAD_DIM
    head_mask = (is_h0, jnp.logical_not(is_h0))
    head_sel = tuple(jnp.where(hm, 1.0, 0.0).astype(BF16) for hm in head_mask)
    ones_cols = tuple(jnp.broadcast_to(sel, (2 * SPAN, LANES)) for sel in head_sel)

    qi = lax.broadcasted_iota(jnp.int32, (SPAN, 2 * SPAN), 0)
    kj = lax.broadcasted_iota(jnp.int32, (SPAN, 2 * SPAN), 1)
    diffs = (qi - kj, qi + SPAN - kj, qi + SPAN - kj)
    extra = (None, kj >= SPAN, None)
    for p, (_, dil) in enumerate(DIL_PATTERNS):
        for hh in range(HEADS_PER_SLAB):
            slope = slopes_ref[hp * HEADS_PER_SLAB + hh]
            for var in range(N_BIAS_VARIANTS):
                diff = diffs[var]
                valid = (diff >= 0) & (diff <= SPAN)
                if extra[var] is not None:
                    valid = valid & extra[var]
                penalty = (-(slope * (float(dil) * LOG2E))) * diff.astype(F32)
                bias_s[p, var, hh] = jnp.where(valid, penalty, -jnp.inf)

    branch_refs = ((qn_ref, kn_ref, vn_ref), (q4_ref, k4_ref, v4_ref), (q16_ref, k16_ref, v16_ref))
    for p in (2, 1, 0):
        dil = DIL_PATTERNS[p][1]
        blocks_per_class = n_blocks // dil
        q_ref, k_ref, v_ref = branch_refs[p]

        for g in range(n_blocks):
            cls, jb = divmod(g, blocks_per_class)
            variant = 0 if g == 0 else (1 if jb == 0 else 2)
            row0 = g * half
            win0 = max(g - 1, 0) * half
            qb = _unpack_rows(q_ref[row0:row0 + half, :])
            kw = _unpack_rows(k_ref[win0:win0 + 2 * half, :])
            vw = _unpack_rows(v_ref[win0:win0 + 2 * half, :])
            ms, es, vs = [], [], []
            for hh in range(HEADS_PER_SLAB):
                qh = qb * head_sel[hh]
                s = lax.dot_general(qh, kw, (((1,), (1,)), ((), ())), preferred_element_type=F32)
                s = s + bias_s[p, variant, hh]
                m = jnp.max(s, axis=1, keepdims=True)
                ms.append(m)
                es.append(jnp.exp2(s - m).astype(BF16))
                vs.append(jnp.concatenate(
                    [jnp.where(head_mask[hh], vw, jnp.zeros_like(vw)), ones_cols[hh]], axis=1))
            ud = jnp.dot(jnp.concatenate(es, axis=1), jnp.concatenate(vs, axis=0),
                         preferred_element_type=F32)
            m_new = jnp.where(is_h0, ms[0], ms[1])
            u_new = ud[:, 0:LANES]
            d_new = ud[:, LANES:]
            start = (dil * SPAN) * jb + cls
            rows = pl.ds(start, SPAN) if dil == 1 else pl.ds(start, SPAN, stride=dil)
            if p == 2:
                m_s[rows, :] = m_new
                d_s[rows, :] = d_new
                u_s[rows, :] = u_new
            else:
                m_old = m_s[rows, :]
                m_all = jnp.maximum(m_old, m_new)
                w_old = jnp.exp2(m_old - m_all)
                w_new = jnp.exp2(m_new - m_all)
                d_all = d_s[rows, :] * w_old + d_new * w_new
                u_all = u_s[rows, :] * w_old + u_new * w_new
                if p == 1:
                    m_s[rows, :] = m_all
                    d_s[rows, :] = d_all
                    u_s[rows, :] = u_all
                else:
                    o_ref[row0:row0 + half, :] = _pack_rows(u_all / d_all)


def _attention(qkv_nat, qkv_r4, qkv_r16, slopes):
    B, half_s, _ = qkv_nat.shape
    S = 2 * half_s
    n_slabs = ATT_WIDTH // LANES

    def slab(part):
        return pl.BlockSpec((None, half_s, LANES), lambda b, h, slopes_ref: (b, 0, part * n_slabs + h))

    qkv_specs = [slab(0), slab(1), slab(2)]
    n_br = len(DIL_PATTERNS)
    return pl.pallas_call(
        functools.partial(_attn_body, seq=S),
        grid_spec=pltpu.PrefetchScalarGridSpec(
            num_scalar_prefetch=1,
            grid=(B, n_slabs),
            in_specs=qkv_specs * n_br,
            out_specs=pl.BlockSpec((None, half_s, LANES), lambda b, h, slopes_ref: (b, 0, h)),
            scratch_shapes=[
                pltpu.VMEM((n_br, N_BIAS_VARIANTS, HEADS_PER_SLAB, SPAN, 2 * SPAN), F32),
                pltpu.VMEM((S, LANES), F32),
                pltpu.VMEM((S, LANES), F32),
                pltpu.VMEM((S, LANES), F32),
            ],
        ),
        out_shape=jax.ShapeDtypeStruct((B, half_s, ATT_WIDTH), U32),
        compiler_params=pltpu.CompilerParams(
            dimension_semantics=("arbitrary", "arbitrary"),
            vmem_limit_bytes=VMEM_LIMIT_BYTES),
        name="dilated_attention",
    )(slopes, qkv_nat, qkv_nat, qkv_nat, qkv_r4, qkv_r4, qkv_r4, qkv_r16, qkv_r16, qkv_r16)


FF_CHUNK = 256
N_STAGE_SLOTS = 4


def _ffn_body(x_ref, att_ref, pool_ref, p_ref, wout_ref, lnf_ref, wup_ref, cw_ref, cb_ref, wdn_ref,
              lnp_ref, wg_ref, wple_ref, lnfin_ref, o_ref, act_s, carry_s, stage_s, *, tm):
    s_idx = pl.program_id(1)

    @pl.when(s_idx == 0)
    def _():
        carry_s[...] = jnp.zeros_like(carry_s)

    th = tm // 2
    halves = (slice(0, th), slice(th, tm))
    packed = (slice(0, th // 2), slice(th // 2, tm // 2))

    hs = []
    for k in range(2):
        hk = x_ref[halves[k], :]
        hk = hk + jnp.dot(_unpack_rows(att_ref[packed[k], :]), wout_ref[0:ATT_WIDTH, :],
                          preferred_element_type=F32)
        hk = hk + jnp.dot(_unpack_rows(pool_ref[packed[k], :]), wout_ref[ATT_WIDTH:, :],
                          preferred_element_type=F32)
        hs.append(hk)
    hns = [_rms_scale(hk, lnf_ref[...]).astype(BF16) for hk in hs]

    def conv(hn, c0, slot):
        cols = slice(c0, c0 + FF_CHUNK)
        pre = jnp.dot(hn, wup_ref[:, cols], preferred_element_type=F32)
        sh1, sh2 = [], []
        for sl in range(FF_CHUNK // LANES):
            lanes = slice(sl * LANES, (sl + 1) * LANES)
            stage_s[slot, sl, pl.ds(0, SUBLANES, stride=2), :] = carry_s[:, c0 + sl * LANES:c0 + (sl + 1) * LANES]
            stage_s[slot, sl, pl.ds(2 * SUBLANES, th, stride=2), :] = pre[:, lanes]
            sh1.append(stage_s[slot, sl, pl.ds(2 * SUBLANES - 2, th, stride=2), :])
            sh2.append(stage_s[slot, sl, pl.ds(2 * SUBLANES - 4, th, stride=2), :])
        carry_s[:, cols] = pre[th - SUBLANES:, :]
        y = cb_ref[:, cols] + cw_ref[0:1, cols] * jnp.concatenate(sh2, axis=1)
        y = y + cw_ref[1:2, cols] * jnp.concatenate(sh1, axis=1)
        return y + cw_ref[2:3, cols] * pre

    n_chunks = D_FF // FF_CHUNK
    for k in range(2):
        for c in range(n_chunks):
            c0 = c * FF_CHUNK
            slot = 2 * (k * n_chunks + c)
            gate = conv(hns[k], c0, slot % N_STAGE_SLOTS)
            val = conv(hns[k], D_FF + c0, (slot + 1) % N_STAGE_SLOTS)
            act_s[halves[k], c0:c0 + FF_CHUNK] = (gate * jax.nn.sigmoid(gate) * val).astype(BF16)

    hs = [hs[k] + jnp.dot(act_s[halves[k], :], wdn_ref[...], preferred_element_type=F32) for k in range(2)]
    for k in range(2):
        hn = _rms_scale(hs[k], lnp_ref[...]).astype(BF16)
        g = jax.nn.sigmoid(jnp.dot(hn, wg_ref[...], preferred_element_type=F32))
        emb = jnp.dot(p_ref[halves[k], :].astype(BF16), wple_ref[...], preferred_element_type=F32)
        o_ref[halves[k], :] = _rms_scale(hs[k] + g * emb, lnfin_ref[...])


def _ffn(x, att, pool, p, w_out, ln_ffn, w_up, conv_w, conv_b, w_down, ln_ple, w_ple_gate, w_ple,
         ln_final, *, tm):
    B, S, D = x.shape
    grid = (B, S // tm)
    row = lambda b, s: (b, s, 0)
    const2 = lambda b, s: (0, 0)

    def resident(shape):
        return pl.BlockSpec(shape, const2, pipeline_mode=pl.Buffered(1))

    return pl.pallas_call(
        functools.partial(_ffn_body, tm=tm),
        grid=grid,
        in_specs=[
            pl.BlockSpec((None, tm, D), row),
            pl.BlockSpec((None, tm // 2, ATT_WIDTH), row),
            pl.BlockSpec((None, tm // 2, POOL_WIDTH), row),
            pl.BlockSpec((None, tm, PLE_DIM), row),
            resident((ATT_WIDTH + POOL_WIDTH, D)),
            resident((1, D)),
            resident((D, 2 * D_FF)),
            resident((CONV_WIDTH, 2 * D_FF)),
            resident((1, 2 * D_FF)),
            resident((D_FF, D)),
            resident((1, D)),
            resident((D, D)),
            resident((PLE_DIM, D)),
            resident((1, D)),
        ],
        out_specs=pl.BlockSpec((None, tm, D), row),
        out_shape=jax.ShapeDtypeStruct((B, S, D), F32),
        scratch_shapes=[
            pltpu.VMEM((tm, D_FF), BF16),
            pltpu.VMEM((SUBLANES, 2 * D_FF), F32),
            pltpu.VMEM((N_STAGE_SLOTS, FF_CHUNK // LANES, 2 * (SUBLANES + tm // 2), LANES), F32),
        ],
        compiler_params=pltpu.CompilerParams(
            dimension_semantics=("arbitrary", "arbitrary"),
            vmem_limit_bytes=VMEM_LIMIT_BYTES),
        name="outproj_ffn_ple",
    )(x, att, pool, p, w_out, ln_ffn.reshape(1, D), w_up, conv_w, conv_b.reshape(1, 2 * D_FF), w_down,
      ln_ple.reshape(1, D), w_ple_gate, w_ple, ln_final.reshape(1, D))


def kernel(x, p, ln_mix, w_in, pool_w, pool_scale, w_out, ln_ffn, w_up, conv_w, conv_b, w_down, ln_ple,
           w_ple_gate, w_ple, ln_final):
    depth = p.shape[0]
    slopes = jnp.exp2(-8.0 * (jnp.arange(N_ATT_HEADS, dtype=F32) + 1.0) / N_ATT_HEADS)
    assert depth == 1, "the final RMSNorm is fused into the last layer's kernel"
    i = 0
    later = (w_out[i], w_up[i], w_down[i], w_ple_gate[i], w_ple[i])
    qkv_nat, qkv_r4, qkv_r16, pool, (w_out_b, w_up_b, w_down_b, w_gate_b, w_ple_b) = _in_proj(
        x, ln_mix[i], w_in[i], pool_w[i], pool_scale[i], later, tm=512)
    att = _attention(qkv_nat, qkv_r4, qkv_r16, slopes)
    return _ffn(x, att, pool, p[i], w_out_b, ln_ffn[i], w_up_b, conv_w[i], conv_b[i], w_down_b,
                ln_ple[i], w_gate_b, w_ple_b, ln_final, tm=512)
```

```python
import functools

import jax
import jax.numpy as jnp
from jax import lax
from jax.experimental import pallas as pl
from jax.experimental.pallas import tpu as pltpu

D_MODEL = 1024
HEAD_DIM = 64
ATT_WIDTH = 512
N_ATT_HEADS = ATT_WIDTH // HEAD_DIM
DIL_PATTERNS = ((128, 1), (512, 4), (2048, 16))
SPAN = 128
POOL_WIDTH = 512
POOL_WINDOWS = (2, 4, 8, 16)
POOL_GROUP_DIM = 128
MAX_POOL_WINDOW = 16
IN_WIDTH = 3 * ATT_WIDTH + POOL_WIDTH
D_FF = 2816
CONV_WIDTH = 3
PLE_DIM = 256
EPS = 1e-6

LANES = 128
SUBLANES = 8
HEADS_PER_SLAB = LANES // HEAD_DIM
VMEM_LIMIT_BYTES = 56 * 1024 * 1024

F32 = jnp.float32
BF16 = jnp.bfloat16
U32 = jnp.uint32


def _pack_rows(t):
    return pltpu.bitcast(t.astype(BF16), U32)


def _unpack_rows(t):
    return pltpu.bitcast(t, BF16)


def _rms_scale(t, g):
    inv = lax.rsqrt(jnp.mean(t * t, axis=-1, keepdims=True) + EPS)
    return t * inv * g


QKV_WIDTH = 3 * ATT_WIDTH
LOG2E = 1.4426950408889634
Q_SCALE = (HEAD_DIM ** -0.5) * LOG2E
N_QKV_SLABS = QKV_WIDTH // LANES


def _in_proj_body(x_ref, g_ref, w_ref, pw_ref, ps_ref, *rest, tm, n_cast):
    cast_in, rest = rest[:n_cast], rest[n_cast:]
    (nat_ref, r4_ref, r16_ref, pool_ref), rest = rest[:4], rest[4:]
    cast_out, (zbuf, z4buf, ucarry) = rest[:n_cast], rest[n_cast:]
    s_idx = pl.program_id(1)
    hw = MAX_POOL_WINDOW

    @pl.when(s_idx == 0)
    def _():
        ucarry[...] = jnp.zeros_like(ucarry)

    for src, dst in zip(cast_in, cast_out):
        dst[...] = src[...].astype(BF16)

    hn = _rms_scale(x_ref[...], g_ref[...]).astype(BF16)
    a = ATT_WIDTH

    u = jnp.dot(hn, w_ref[:, 3 * a:], preferred_element_type=F32)
    e = jnp.concatenate([ucarry[...], u], axis=0)
    ucarry[...] = u[tm - hw:, :]
    gd = POOL_GROUP_DIM
    s2 = e + pltpu.roll(e, 1, 0)
    s4 = s2[:, gd:] + pltpu.roll(s2[:, gd:], 2, 0)
    s8 = s4[:, gd:] + pltpu.roll(s4[:, gd:], 4, 0)
    s16 = s8[:, gd:] + pltpu.roll(s8[:, gd:], 8, 0)
    sums = (s2[hw:, 0:gd], s4[hw:, 0:gd], s8[hw:, 0:gd], s16[hw:, :])
    t1 = (s_idx * tm + lax.broadcasted_iota(jnp.int32, (tm, 1), 0) + 1).astype(F32)
    dlts = []
    for g, w in enumerate(POOL_WINDOWS):
        inv_cnt = 1.0 / jnp.minimum(t1, float(w))
        dlts.append((sums[g] * inv_cnt - u[:, g * gd:(g + 1) * gd]).astype(BF16))

    slabs_per_part = a // LANES
    n4 = tm // 4
    for part in range(3):
        z = jnp.dot(hn, w_ref[:, part * a:(part + 1) * a], preferred_element_type=F32)
        if part == 0:
            z = z * Q_SCALE
        nat_ref[:, part * a:(part + 1) * a] = _pack_rows(z)
        for sl in range(slabs_per_part):
            j = part * slabs_per_part + sl
            cols = slice(j * LANES, (j + 1) * LANES)
            zbuf[j] = z[:, sl * LANES:(sl + 1) * LANES]
            for c in range(4):
                cls4 = zbuf[j, pl.ds(c, n4, stride=4), :]
                r4_ref[c, :, cols] = _pack_rows(cls4)
                z4buf[j, c * n4:(c + 1) * n4, :] = cls4
            for c in range(4):
                for b in range(4):
                    cls16 = z4buf[j, pl.ds(c * n4 + b, tm // 16, stride=4), :]
                    r16_ref[4 * b + c, :, cols] = _pack_rows(cls16)

    for g in range(len(POOL_WINDOWS)):
        y = jnp.dot(dlts[g], pw_ref[g], preferred_element_type=F32)
        pool_ref[:, g * gd:(g + 1) * gd] = _pack_rows(y * ps_ref[:, g * gd:(g + 1) * gd])


def _cast_block_rows(rows, n_steps):
    br = 2 * SUBLANES
    while rows % br or rows // br > n_steps:
        br += 2 * SUBLANES
    return br


def _in_proj(x, ln_mix, w_in, pool_w, pool_scale, later_weights, *, tm):
    B, S, D = x.shape
    grid = (B, S // tm)
    n_steps = B * (S // tm)
    cast_specs = []
    for w in later_weights:
        br = _cast_block_rows(w.shape[0], n_steps)
        last = w.shape[0] // br - 1
        cast_specs.append(pl.BlockSpec(
            (br, w.shape[1]), lambda b, s, last=last: (jnp.minimum(b * (S // tm) + s, last), 0)))
    row = lambda b, s: (b, s, 0)
    cls_row = lambda b, s: (b, 0, s, 0)
    const2 = lambda b, s: (0, 0)
    const3 = lambda b, s: (0, 0, 0)
    nat, r4, r16, pool, *casted = pl.pallas_call(
        functools.partial(_in_proj_body, tm=tm, n_cast=len(later_weights)),
        grid=grid,
        in_specs=[
            pl.BlockSpec((None, tm, D), row),
            pl.BlockSpec((1, D), const2),
            pl.BlockSpec((D, IN_WIDTH), const2),
            pl.BlockSpec((len(POOL_WINDOWS), POOL_GROUP_DIM, POOL_GROUP_DIM), const3),
            pl.BlockSpec((1, POOL_WIDTH), const2),
        ] + cast_specs,
        out_specs=[
            pl.BlockSpec((None, tm // 2, QKV_WIDTH), row),
            pl.BlockSpec((None, 4, tm // 8, QKV_WIDTH), cls_row),
            pl.BlockSpec((None, 16, tm // 32, QKV_WIDTH), cls_row),
            pl.BlockSpec((None, tm // 2, POOL_WIDTH), row),
        ] + cast_specs,
        out_shape=[
            jax.ShapeDtypeStruct((B, S // 2, QKV_WIDTH), U32),
            jax.ShapeDtypeStruct((B, 4, S // 8, QKV_WIDTH), U32),
            jax.ShapeDtypeStruct((B, 16, S // 32, QKV_WIDTH), U32),
            jax.ShapeDtypeStruct((B, S // 2, POOL_WIDTH), U32),
        ] + [jax.ShapeDtypeStruct(w.shape, BF16) for w in later_weights],
        scratch_shapes=[
            pltpu.VMEM((N_QKV_SLABS, tm, LANES), F32),
            pltpu.VMEM((N_QKV_SLABS, tm, LANES), F32),
            pltpu.VMEM((MAX_POOL_WINDOW, POOL_WIDTH), F32),
        ],
        compiler_params=pltpu.CompilerParams(
            dimension_semantics=("arbitrary", "arbitrary"),
            vmem_limit_bytes=VMEM_LIMIT_BYTES),
        name="in_proj_pool",
    )(x, ln_mix.reshape(1, D), w_in.astype(BF16), pool_w.astype(BF16), pool_scale.reshape(1, POOL_WIDTH),
      *later_weights)
    return nat, r4.reshape(B, S // 2, QKV_WIDTH), r16.reshape(B, S // 2, QKV_WIDTH), pool, casted


N_BIAS_VARIANTS = 3


def _attn_body(slopes_ref, qn_ref, kn_ref, vn_ref, q4_ref, k4_ref, v4_ref, q16_ref, k16_ref, v16_ref,
               o_ref, bias_s, m_s, d_s, u_s, *, seq):
    hp = pl.program_id(1)
    n_blocks = seq // SPAN
    half = SPAN // 2
    lane = lax.broadcasted_iota(jnp.int32, (1, LANES), 1)
    is_h0 = lane < HEAD_DIM
    head_mask = (is_h0, jnp.logical_not(is_h0))
    head_sel = tuple(jnp.where(hm, 1.0, 0.0).astype(BF16) for hm in head_mask)
    ones_cols = tuple(jnp.broadcast_to(sel, (2 * SPAN, LANES)) for sel in head_sel)

    qi = lax.broadcasted_iota(jnp.int32, (SPAN, 2 * SPAN), 0)
    kj = lax.broadcasted_iota(jnp.int32, (SPAN, 2 * SPAN), 1)
    diffs = (qi - kj, qi + SPAN - kj, qi + SPAN - kj)
    extra = (None, kj >= SPAN, None)
    for p, (_, dil) in enumerate(DIL_PATTERNS):
        for hh in range(HEADS_PER_SLAB):
            slope = slopes_ref[hp * HEADS_PER_SLAB + hh]
            for var in range(N_BIAS_VARIANTS):
                diff = diffs[var]
                valid = (diff >= 0) & (diff <= SPAN)
                if extra[var] is not None:
                    valid = valid & extra[var]
                penalty = (-(slope * (float(dil) * LOG2E))) * diff.astype(F32)
                bias_s[p, var, hh] = jnp.where(valid, penalty, -jnp.inf)

    branch_refs = ((qn_ref, kn_ref, vn_ref), (q4_ref, k4_ref, v4_ref), (q16_ref, k16_ref, v16_ref))
    for p in (2, 1, 0):
        dil = DIL_PATTERNS[p][1]
        blocks_per_class = n_blocks // dil
        q_ref, k_ref, v_ref = branch_refs[p]

        for g in range(n_blocks):
            cls, jb = divmod(g, blocks_per_class)
            variant = 0 if g == 0 else (1 if jb == 0 else 2)
            row0 = g * half
            win0 = max(g - 1, 0) * half
            qb = _unpack_rows(q_ref[row0:row0 + half, :])
            kw = _unpack_rows(k_ref[win0:win0 + 2 * half, :])
            vw = _unpack_rows(v_ref[win0:win0 + 2 * half, :])
            ms, es, vs = [], [], []
            for hh in range(HEADS_PER_SLAB):
                qh = qb * head_sel[hh]
                s = lax.dot_general(qh, kw, (((1,), (1,)), ((), ())), preferred_element_type=F32)
                s = s + bias_s[p, variant, hh]
                m = jnp.max(s, axis=1, keepdims=True)
                ms.append(m)
                es.append(jnp.exp2(s - m).astype(BF16))
                vs.append(jnp.concatenate(
                    [jnp.where(head_mask[hh], vw, jnp.zeros_like(vw)), ones_cols[hh]], axis=1))
            ud = jnp.dot(jnp.concatenate(es, axis=1), jnp.concatenate(vs, axis=0),
                         preferred_element_type=F32)
            m_new = jnp.where(is_h0, ms[0], ms[1])
            u_new = ud[:, 0:LANES]
            d_new = ud[:, LANES:]
            start = (dil * SPAN) * jb + cls
            rows = pl.ds(start, SPAN) if dil == 1 else pl.ds(start, SPAN, stride=dil)
            if p == 2:
                m_s[rows, :] = m_new
                d_s[rows, :] = d_new
                u_s[rows, :] = u_new
            else:
                m_old = m_s[rows, :]
                m_all = jnp.maximum(m_old, m_new)
                w_old = jnp.exp2(m_old - m_all)
                w_new = jnp.exp2(m_new - m_all)
                d_all = d_s[rows, :] * w_old + d_new * w_new
                u_all = u_s[rows, :] * w_old + u_new * w_new
                if p == 1:
                    m_s[rows, :] = m_all
                    d_s[rows, :] = d_all
                    u_s[rows, :] = u_all
                else:
                    o_ref[row0:row0 + half, :] = _pack_rows(u_all / d_all)


def _attention(qkv_nat, qkv_r4, qkv_r16, slopes):
    B, half_s, _ = qkv_nat.shape
    S = 2 * half_s
    n_slabs = ATT_WIDTH // LANES

    def slab(part):
        return pl.BlockSpec((None, half_s, LANES), lambda b, h, slopes_ref: (b, 0, part * n_slabs + h))

    qkv_specs = [slab(0), slab(1), slab(2)]
    n_br = len(DIL_PATTERNS)
    return pl.pallas_call(
        functools.partial(_attn_body, seq=S),
        grid_spec=pltpu.PrefetchScalarGridSpec(
            num_scalar_prefetch=1,
            grid=(B, n_slabs),
            in_specs=qkv_specs * n_br,
            out_specs=pl.BlockSpec((None, half_s, LANES), lambda b, h, slopes_ref: (b, 0, h)),
            scratch_shapes=[
                pltpu.VMEM((n_br, N_BIAS_VARIANTS, HEADS_PER_SLAB, SPAN, 2 * SPAN), F32),
                pltpu.VMEM((S, LANES), F32),
                pltpu.VMEM((S, LANES), F32),
                pltpu.VMEM((S, LANES), F32),
            ],
        ),
        out_shape=jax.ShapeDtypeStruct((B, half_s, ATT_WIDTH), U32),
        compiler_params=pltpu.CompilerParams(
            dimension_semantics=("arbitrary", "arbitrary"),
            vmem_limit_bytes=VMEM_LIMIT_BYTES),
        name="dilated_attention",
    )(slopes, qkv_nat, qkv_nat, qkv_nat, qkv_r4, qkv_r4, qkv_r4, qkv_r16, qkv_r16, qkv_r16)


FF_CHUNK = 256
N_STAGE_SLOTS = 4
STAGE_PITCH = 3


def _ffn_body(x_ref, att_ref, pool_ref, p_ref, wout_ref, lnf_ref, wup_ref, cw_ref, cb_ref, wdn_ref,
              lnp_ref, wg_ref, wple_ref, lnfin_ref, o_ref, act_s, carry_s, stage_s, *, tm):
    s_idx = pl.program_id(1)

    @pl.when(s_idx == 0)
    def _():
        carry_s[...] = jnp.zeros_like(carry_s)

    th = tm // 2
    halves = (slice(0, th), slice(th, tm))
    packed = (slice(0, th // 2), slice(th // 2, tm // 2))

    hs = []
    for k in range(2):
        hk = x_ref[halves[k], :]
        hk = hk + jnp.dot(_unpack_rows(att_ref[packed[k], :]), wout_ref[0:ATT_WIDTH, :],
                          preferred_element_type=F32)
        hk = hk + jnp.dot(_unpack_rows(pool_ref[packed[k], :]), wout_ref[ATT_WIDTH:, :],
                          preferred_element_type=F32)
        hs.append(hk)
    hns = [_rms_scale(hk, lnf_ref[...]).astype(BF16) for hk in hs]

    def conv(hn, c0, slot):
        cols = slice(c0, c0 + FF_CHUNK)
        pre = jnp.dot(hn, wup_ref[:, cols], preferred_element_type=F32)
        sh1, sh2 = [], []
        sp = STAGE_PITCH
        for sl in range(FF_CHUNK // LANES):
            lanes = slice(sl * LANES, (sl + 1) * LANES)
            stage_s[slot, sl, pl.ds(0, SUBLANES, stride=sp), :] = carry_s[:, c0 + sl * LANES:c0 + (sl + 1) * LANES]
            stage_s[slot, sl, pl.ds(sp * SUBLANES, th, stride=sp), :] = pre[:, lanes]
            sh1.append(stage_s[slot, sl, pl.ds(sp * (SUBLANES - 1), th, stride=sp), :])
            sh2.append(stage_s[slot, sl, pl.ds(sp * (SUBLANES - 2), th, stride=sp), :])
        carry_s[:, cols] = pre[th - SUBLANES:, :]
        y = cb_ref[:, cols] + cw_ref[0:1, cols] * jnp.concatenate(sh2, axis=1)
        y = y + cw_ref[1:2, cols] * jnp.concatenate(sh1, axis=1)
        return y + cw_ref[2:3, cols] * pre

    n_chunks = D_FF // FF_CHUNK
    for k in range(2):
        for c in range(n_chunks):
            c0 = c * FF_CHUNK
            slot = 2 * (k * n_chunks + c)
            gate = conv(hns[k], c0, slot % N_STAGE_SLOTS)
            val = conv(hns[k], D_FF + c0, (slot + 1) % N_STAGE_SLOTS)
            act_s[halves[k], c0:c0 + FF_CHUNK] = (gate * jax.nn.sigmoid(gate) * val).astype(BF16)

    hs = [hs[k] + jnp.dot(act_s[halves[k], :], wdn_ref[...], preferred_element_type=F32) for k in range(2)]
    for k in range(2):
        hn = _rms_scale(hs[k], lnp_ref[...]).astype(BF16)
        g = jax.nn.sigmoid(jnp.dot(hn, wg_ref[...], preferred_element_type=F32))
        emb = jnp.dot(p_ref[halves[k], :].astype(BF16), wple_ref[...], preferred_element_type=F32)
        o_ref[halves[k], :] = _rms_scale(hs[k] + g * emb, lnfin_ref[...])


def _ffn(x, att, pool, p, w_out, ln_ffn, w_up, conv_w, conv_b, w_down, ln_ple, w_ple_gate, w_ple,
         ln_final, *, tm):
    B, S, D = x.shape
    grid = (B, S // tm)
    row = lambda b, s: (b, s, 0)
    const2 = lambda b, s: (0, 0)

    def resident(shape):
        return pl.BlockSpec(shape, const2, pipeline_mode=pl.Buffered(1))

    return pl.pallas_call(
        functools.partial(_ffn_body, tm=tm),
        grid=grid,
        in_specs=[
            pl.BlockSpec((None, tm, D), row),
            pl.BlockSpec((None, tm // 2, ATT_WIDTH), row),
            pl.BlockSpec((None, tm // 2, POOL_WIDTH), row),
            pl.BlockSpec((None, tm, PLE_DIM), row),
            resident((ATT_WIDTH + POOL_WIDTH, D)),
            resident((1, D)),
            resident((D, 2 * D_FF)),
            resident((CONV_WIDTH, 2 * D_FF)),
            resident((1, 2 * D_FF)),
            resident((D_FF, D)),
            resident((1, D)),
            resident((D, D)),
            resident((PLE_DIM, D)),
            resident((1, D)),
        ],
        out_specs=pl.BlockSpec((None, tm, D), row),
        out_shape=jax.ShapeDtypeStruct((B, S, D), F32),
        scratch_shapes=[
            pltpu.VMEM((tm, D_FF), BF16),
            pltpu.VMEM((SUBLANES, 2 * D_FF), F32),
            pltpu.VMEM((N_STAGE_SLOTS, FF_CHUNK // LANES, STAGE_PITCH * (SUBLANES + tm // 2), LANES), F32),
        ],
        compiler_params=pltpu.CompilerParams(
            dimension_semantics=("arbitrary", "arbitrary"),
            vmem_limit_bytes=VMEM_LIMIT_BYTES),
        name="outproj_ffn_ple",
    )(x, att, pool, p, w_out, ln_ffn.reshape(1, D), w_up, conv_w, conv_b.reshape(1, 2 * D_FF), w_down,
      ln_ple.reshape(1, D), w_ple_gate, w_ple, ln_final.reshape(1, D))


def kernel(x, p, ln_mix, w_in, pool_w, pool_scale, w_out, ln_ffn, w_up, conv_w, conv_b, w_down, ln_ple,
           w_ple_gate, w_ple, ln_final):
    depth = p.shape[0]
    slopes = jnp.exp2(-8.0 * (jnp.arange(N_ATT_HEADS, dtype=F32) + 1.0) / N_ATT_HEADS)
    assert depth == 1, "the final RMSNorm is fused into the last layer's kernel"
    i = 0
    later = (w_out[i], w_up[i], w_down[i], w_ple_gate[i], w_ple[i])
    qkv_nat, qkv_r4, qkv_r16, pool, (w_out_b, w_up_b, w_down_b, w_gate_b, w_ple_b) = _in_proj(
        x, ln_mix[i], w_in[i], pool_w[i], pool_scale[i], later, tm=512)
    att = _attention(qkv_nat, qkv_r4, qkv_r16, slopes)
    return _ffn(x, att, pool, p[i], w_out_b, ln_ffn[i], w_up_b, conv_w[i], conv_b[i], w_down_b,
                ln_ple[i], w_gate_b, w_ple_b, ln_final, tm=512)
```

```python
import functools

import jax
import jax.numpy as jnp
from jax import lax
from jax.experimental import pallas as pl
from jax.experimental.pallas import tpu as pltpu

D_MODEL = 1024
HEAD_DIM = 64
ATT_WIDTH = 512
N_ATT_HEADS = ATT_WIDTH // HEAD_DIM
DIL_PATTERNS = ((128, 1), (512, 4), (2048, 16))
SPAN = 128
POOL_WIDTH = 512
POOL_WINDOWS = (2, 4, 8, 16)
POOL_GROUP_DIM = 128
MAX_POOL_WINDOW = 16
IN_WIDTH = 3 * ATT_WIDTH + POOL_WIDTH
D_FF = 2816
CONV_WIDTH = 3
PLE_DIM = 256
EPS = 1e-6

LANES = 128
SUBLANES = 8
HEADS_PER_SLAB = LANES // HEAD_DIM
VMEM_LIMIT_BYTES = 56 * 1024 * 1024

F32 = jnp.float32
BF16 = jnp.bfloat16
U32 = jnp.uint32


def _pack_rows(t):
    return pltpu.bitcast(t.astype(BF16), U32)


def _unpack_rows(t):
    return pltpu.bitcast(t, BF16)


def _rms_scale(t, g):
    inv = lax.rsqrt(jnp.mean(t * t, axis=-1, keepdims=True) + EPS)
    return t * inv * g


QKV_WIDTH = 3 * ATT_WIDTH
LOG2E = 1.4426950408889634
Q_SCALE = (HEAD_DIM ** -0.5) * LOG2E
N_QKV_SLABS = QKV_WIDTH // LANES


def _in_proj_body(x_ref, g_ref, w_ref, pw_ref, ps_ref, *rest, tm, n_cast):
    cast_in, rest = rest[:n_cast], rest[n_cast:]
    (nat_ref, r4_ref, r16_ref, pool_ref), rest = rest[:4], rest[4:]
    cast_out, (zbuf, z4buf, ucarry) = rest[:n_cast], rest[n_cast:]
    s_idx = pl.program_id(1)
    hw = MAX_POOL_WINDOW

    @pl.when(s_idx == 0)
    def _():
        ucarry[...] = jnp.zeros_like(ucarry)

    for src, dst in zip(cast_in, cast_out):
        dst[...] = src[...].astype(BF16)

    hn = _rms_scale(x_ref[...], g_ref[...]).astype(BF16)
    a = ATT_WIDTH

    u = jnp.dot(hn, w_ref[:, 3 * a:], preferred_element_type=F32)
    e = jnp.concatenate([ucarry[...], u], axis=0)
    ucarry[...] = u[tm - hw:, :]
    gd = POOL_GROUP_DIM
    s2 = e + pltpu.roll(e, 1, 0)
    s4 = s2[:, gd:] + pltpu.roll(s2[:, gd:], 2, 0)
    s8 = s4[:, gd:] + pltpu.roll(s4[:, gd:], 4, 0)
    s16 = s8[:, gd:] + pltpu.roll(s8[:, gd:], 8, 0)
    sums = (s2[hw:, 0:gd], s4[hw:, 0:gd], s8[hw:, 0:gd], s16[hw:, :])
    t1 = (s_idx * tm + lax.broadcasted_iota(jnp.int32, (tm, 1), 0) + 1).astype(F32)
    dlts = []
    for g, w in enumerate(POOL_WINDOWS):
        inv_cnt = 1.0 / jnp.minimum(t1, float(w))
        dlts.append((sums[g] * inv_cnt - u[:, g * gd:(g + 1) * gd]).astype(BF16))

    slabs_per_part = a // LANES
    n4 = tm // 4
    for part in range(3):
        z = jnp.dot(hn, w_ref[:, part * a:(part + 1) * a], preferred_element_type=F32)
        if part == 0:
            z = z * Q_SCALE
        nat_ref[:, part * a:(part + 1) * a] = _pack_rows(z)
        for sl in range(slabs_per_part):
            j = part * slabs_per_part + sl
            cols = slice(j * LANES, (j + 1) * LANES)
            zbuf[j] = z[:, sl * LANES:(sl + 1) * LANES]
            for c in range(4):
                cls4 = zbuf[j, pl.ds(c, n4, stride=4), :]
                r4_ref[c, :, cols] = _pack_rows(cls4)
                z4buf[j, c * n4:(c + 1) * n4, :] = cls4
            for c in range(4):
                for b in range(4):
                    cls16 = z4buf[j, pl.ds(c * n4 + b, tm // 16, stride=4), :]
                    r16_ref[4 * b + c, :, cols] = _pack_rows(cls16)

    for g in range(len(POOL_WINDOWS)):
        y = jnp.dot(dlts[g], pw_ref[g], preferred_element_type=F32)
        pool_ref[:, g * gd:(g + 1) * gd] = _pack_rows(y * ps_ref[:, g * gd:(g + 1) * gd])


def _cast_block_rows(rows, n_steps):
    br = 2 * SUBLANES
    while rows % br or rows // br > n_steps:
        br += 2 * SUBLANES
    return br


def _in_proj(x, ln_mix, w_in, pool_w, pool_scale, later_weights, *, tm):
    B, S, D = x.shape
    grid = (B, S // tm)
    n_steps = B * (S // tm)
    cast_specs = []
    for w in later_weights:
        br = _cast_block_rows(w.shape[0], n_steps)
        last = w.shape[0] // br - 1
        cast_specs.append(pl.BlockSpec(
            (br, w.shape[1]), lambda b, s, last=last: (jnp.minimum(b * (S // tm) + s, last), 0)))
    row = lambda b, s: (b, s, 0)
    cls_row = lambda b, s: (b, 0, s, 0)
    const2 = lambda b, s: (0, 0)
    const3 = lambda b, s: (0, 0, 0)
    nat, r4, r16, pool, *casted = pl.pallas_call(
        functools.partial(_in_proj_body, tm=tm, n_cast=len(later_weights)),
        grid=grid,
        in_specs=[
            pl.BlockSpec((None, tm, D), row),
            pl.BlockSpec((1, D), const2),
            pl.BlockSpec((D, IN_WIDTH), const2),
            pl.BlockSpec((len(POOL_WINDOWS), POOL_GROUP_DIM, POOL_GROUP_DIM), const3),
            pl.BlockSpec((1, POOL_WIDTH), const2),
        ] + cast_specs,
        out_specs=[
            pl.BlockSpec((None, tm // 2, QKV_WIDTH), row),
            pl.BlockSpec((None, 4, tm // 8, QKV_WIDTH), cls_row),
            pl.BlockSpec((None, 16, tm // 32, QKV_WIDTH), cls_row),
            pl.BlockSpec((None, tm // 2, POOL_WIDTH), row),
        ] + cast_specs,
        out_shape=[
            jax.ShapeDtypeStruct((B, S // 2, QKV_WIDTH), U32),
            jax.ShapeDtypeStruct((B, 4, S // 8, QKV_WIDTH), U32),
            jax.ShapeDtypeStruct((B, 16, S // 32, QKV_WIDTH), U32),
            jax.ShapeDtypeStruct((B, S // 2, POOL_WIDTH), U32),
        ] + [jax.ShapeDtypeStruct(w.shape, BF16) for w in later_weights],
        scratch_shapes=[
            pltpu.VMEM((N_QKV_SLABS, tm, LANES), F32),
            pltpu.VMEM((N_QKV_SLABS, tm, LANES), F32),
            pltpu.VMEM((MAX_POOL_WINDOW, POOL_WIDTH), F32),
        ],
        compiler_params=pltpu.CompilerParams(
            dimension_semantics=("arbitrary", "arbitrary"),
            vmem_limit_bytes=VMEM_LIMIT_BYTES),
        name="in_proj_pool",
    )(x, ln_mix.reshape(1, D), w_in.astype(BF16), pool_w.astype(BF16), pool_scale.reshape(1, POOL_WIDTH),
      *later_weights)
    return nat, r4.reshape(B, S // 2, QKV_WIDTH), r16.reshape(B, S // 2, QKV_WIDTH), pool, casted


N_BIAS_VARIANTS = 3


def _attn_body(slopes_ref, qn_ref, kn_ref, vn_ref, q4_ref, k4_ref, v4_ref, q16_ref, k16_ref, v16_ref,
               o_ref, bias_s, m_s, d_s, u_s, *, seq):
    hp = pl.program_id(1)
    n_blocks = seq // SPAN
    half = SPAN // 2
    lane = lax.broadcasted_iota(jnp.int32, (1, LANES), 1)
    is_h0 = lane < HEAD_DIM
    head_mask = (is_h0, jnp.logical_not(is_h0))
    head_sel = tuple(jnp.where(hm, 1.0, 0.0).astype(BF16) for hm in head_mask)
    ones_cols = tuple(jnp.broadcast_to(sel, (2 * SPAN, LANES)) for sel in head_sel)

    qi = lax.broadcasted_iota(jnp.int32, (SPAN, 2 * SPAN), 0)
    kj = lax.broadcasted_iota(jnp.int32, (SPAN, 2 * SPAN), 1)
    diffs = (qi - kj, qi + SPAN - kj, qi + SPAN - kj)
    extra = (None, kj >= SPAN, None)
    for p, (_, dil) in enumerate(DIL_PATTERNS):
        for hh in range(HEADS_PER_SLAB):
            slope = slopes_ref[hp * HEADS_PER_SLAB + hh]
            for var in range(N_BIAS_VARIANTS):
                diff = diffs[var]
                valid = (diff >= 0) & (diff <= SPAN)
                if extra[var] is not None:
                    valid = valid & extra[var]
                penalty = (-(slope * (float(dil) * LOG2E))) * diff.astype(F32)
                bias_s[p, var, hh] = jnp.where(valid, penalty, -jnp.inf)

    branch_refs = ((qn_ref, kn_ref, vn_ref), (q4_ref, k4_ref, v4_ref), (q16_ref, k16_ref, v16_ref))
    for p in (2, 1, 0):
        dil = DIL_PATTERNS[p][1]
        blocks_per_class = n_blocks // dil
        q_ref, k_ref, v_ref = branch_refs[p]

        for g in range(n_blocks):
            cls, jb = divmod(g, blocks_per_class)
            variant = 0 if g == 0 else (1 if jb == 0 else 2)
            row0 = g * half
            win0 = max(g - 1, 0) * half
            qb = _unpack_rows(q_ref[row0:row0 + half, :])
            kw = _unpack_rows(k_ref[win0:win0 + 2 * half, :])
            vw = _unpack_rows(v_ref[win0:win0 + 2 * half, :])
            ms, es, vs = [], [], []
            for hh in range(HEADS_PER_SLAB):
                qh = qb * head_sel[hh]
                s = lax.dot_general(qh, kw, (((1,), (1,)), ((), ())), preferred_element_type=F32)
                s = s + bias_s[p, variant, hh]
                m = jnp.max(s, axis=1, keepdims=True)
                ms.append(m)
                es.append(jnp.exp2(s - m).astype(BF16))
                vs.append(jnp.concatenate(
                    [jnp.where(head_mask[hh], vw, jnp.zeros_like(vw)), ones_cols[hh]], axis=1))
            ud = jnp.dot(jnp.concatenate(es, axis=1), jnp.concatenate(vs, axis=0),
                         preferred_element_type=F32)
            m_new = jnp.where(is_h0, ms[0], ms[1])
            u_new = ud[:, 0:LANES]
            d_new = ud[:, LANES:]
            start = (dil * SPAN) * jb + cls
            rows = pl.ds(start, SPAN) if dil == 1 else pl.ds(start, SPAN, stride=dil)
            if p == 2:
                m_s[rows, :] = m_new
                d_s[rows, :] = d_new
                u_s[rows, :] = u_new
            else:
                m_old = m_s[rows, :]
                m_all = jnp.maximum(m_old, m_new)
                w_old = jnp.exp2(m_old - m_all)
                w_new = jnp.exp2(m_new - m_all)
                d_all = d_s[rows, :] * w_old + d_new * w_new
                u_all = u_s[rows, :] * w_old + u_new * w_new
                if p == 1:
                    m_s[rows, :] = m_all
                    d_s[rows, :] = d_all
                    u_s[rows, :] = u_all
                else:
                    o_ref[row0:row0 + half, :] = _pack_rows(u_all / d_all)


def _attention(qkv_nat, qkv_r4, qkv_r16, slopes):
    B, half_s, _ = qkv_nat.shape
    S = 2 * half_s
    n_slabs = ATT_WIDTH // LANES

    def slab(part):
        return pl.BlockSpec((None, half_s, LANES), lambda b, h, slopes_ref: (b, 0, part * n_slabs + h))

    qkv_specs = [slab(0), slab(1), slab(2)]
    n_br = len(DIL_PATTERNS)
    return pl.pallas_call(
        functools.partial(_attn_body, seq=S),
        grid_spec=pltpu.PrefetchScalarGridSpec(
            num_scalar_prefetch=1,
            grid=(B, n_slabs),
            in_specs=qkv_specs * n_br,
            out_specs=pl.BlockSpec((None, half_s, LANES), lambda b, h, slopes_ref: (b, 0, h)),
            scratch_shapes=[
                pltpu.VMEM((n_br, N_BIAS_VARIANTS, HEADS_PER_SLAB, SPAN, 2 * SPAN), F32),
                pltpu.VMEM((S, LANES), F32),
                pltpu.VMEM((S, LANES), F32),
                pltpu.VMEM((S, LANES), F32),
            ],
        ),
        out_shape=jax.ShapeDtypeStruct((B, half_s, ATT_WIDTH), U32),
        compiler_params=pltpu.CompilerParams(
            dimension_semantics=("arbitrary", "arbitrary"),
            vmem_limit_bytes=VMEM_LIMIT_BYTES),
        name="dilated_attention",
    )(slopes, qkv_nat, qkv_nat, qkv_nat, qkv_r4, qkv_r4, qkv_r4, qkv_r16, qkv_r16, qkv_r16)


FF_CHUNK = 256
N_STAGE_SLOTS = 4
STAGE_PITCH = 3


def _ffn_body(x_ref, att_ref, pool_ref, p_ref, wout_ref, lnf_ref, wup_ref, cw_ref, cb_ref, wdn_ref,
              lnp_ref, wg_ref, wple_ref, lnfin_ref, o_ref, act_s, carry_s, stage_s, *, tm):
    s_idx = pl.program_id(1)

    @pl.when(s_idx == 0)
    def _():
        carry_s[...] = jnp.zeros_like(carry_s)

    th = tm // 2
    halves = (slice(0, th), slice(th, tm))
    packed = (slice(0, th // 2), slice(th // 2, tm // 2))

    hs = []
    for k in range(2):
        hk = x_ref[halves[k], :]
        hk = hk + jnp.dot(_unpack_rows(att_ref[packed[k], :]), wout_ref[0:ATT_WIDTH, :],
                          preferred_element_type=F32)
        hk = hk + jnp.dot(_unpack_rows(pool_ref[packed[k], :]), wout_ref[ATT_WIDTH:, :],
                          preferred_element_type=F32)
        hs.append(hk)
    hns = [_rms_scale(hk, lnf_ref[...]).astype(BF16) for hk in hs]

    def conv(hn, c0, slot):
        cols = slice(c0, c0 + FF_CHUNK)
        pre = jnp.dot(hn, wup_ref[:, cols], preferred_element_type=F32)
        sh1, sh2 = [], []
        sp = STAGE_PITCH
        for sl in range(FF_CHUNK // LANES):
            lanes = slice(sl * LANES, (sl + 1) * LANES)
            stage_s[slot, sl, pl.ds(0, SUBLANES, stride=sp), :] = carry_s[:, c0 + sl * LANES:c0 + (sl + 1) * LANES]
            stage_s[slot, sl, pl.ds(sp * SUBLANES, th, stride=sp), :] = pre[:, lanes]
            sh1.append(stage_s[slot, sl, pl.ds(sp * (SUBLANES - 1), th, stride=sp), :])
            sh2.append(stage_s[slot, sl, pl.ds(sp * (SUBLANES - 2), th, stride=sp), :])
        carry_s[:, cols] = pre[th - SUBLANES:, :]
        y = cb_ref[:, cols] + cw_ref[0:1, cols] * jnp.concatenate(sh2, axis=1)
        y = y + cw_ref[1:2, cols] * jnp.concatenate(sh1, axis=1)
        return y + cw_ref[2:3, cols] * pre

    n_chunks = D_FF // FF_CHUNK
    for k in range(2):
        for c in range(n_chunks):
            c0 = c * FF_CHUNK
            slot = 2 * (k * n_chunks + c)
            gate = conv(hns[k], c0, slot % N_STAGE_SLOTS)
            val = conv(hns[k], D_FF + c0, (slot + 1) % N_STAGE_SLOTS)
            act_s[halves[k], c0:c0 + FF_CHUNK] = (gate * jax.nn.sigmoid(gate) * val).astype(BF16)

    hs = [hs[k] + jnp.dot(act_s[halves[k], :], wdn_ref[...], preferred_element_type=F32) for k in range(2)]
    for k in range(2):
        hn = _rms_scale(hs[k], lnp_ref[...]).astype(BF16)
        g = jax.nn.sigmoid(jnp.dot(hn, wg_ref[...], preferred_element_type=F32))
        emb = jnp.dot(p_ref[halves[k], :].astype(BF16), wple_ref[...], preferred_element_type=F32)
        o_ref[halves[k], :] = _rms_scale(hs[k] + g * emb, lnfin_ref[...])


def _ffn(x, att, pool, p, w_out, ln_ffn, w_up, conv_w, conv_b, w_down, ln_ple, w_ple_gate, w_ple,
         ln_final, *, tm):
    B, S, D = x.shape
    grid = (B, S // tm)
    row = lambda b, s: (b, s, 0)
    const2 = lambda b, s: (0, 0)

    def resident(shape):
        return pl.BlockSpec(shape, const2, pipeline_mode=pl.Buffered(1))

    return pl.pallas_call(
        functools.partial(_ffn_body, tm=tm),
        grid=grid,
        in_specs=[
            pl.BlockSpec((None, tm, D), row),
            pl.BlockSpec((None, tm // 2, ATT_WIDTH), row),
            pl.BlockSpec((None, tm // 2, POOL_WIDTH), row),
            pl.BlockSpec((None, tm, PLE_DIM), row),
            resident((ATT_WIDTH + POOL_WIDTH, D)),
            resident((1, D)),
            resident((D, 2 * D_FF)),
            resident((CONV_WIDTH, 2 * D_FF)),
            resident((1, 2 * D_FF)),
            resident((D_FF, D)),
            resident((1, D)),
            resident((D, D)),
            resident((PLE_DIM, D)),
            resident((1, D)),
        ],
        out_specs=pl.BlockSpec((None, tm, D), row),
        out_shape=jax.ShapeDtypeStruct((B, S, D), F32),
        scratch_shapes=[
            pltpu.VMEM((tm, D_FF), BF16),
            pltpu.VMEM((SUBLANES, 2 * D_FF), F32),
            pltpu.VMEM((N_STAGE_SLOTS, FF_CHUNK // LANES, STAGE_PITCH * (SUBLANES + tm // 2), LANES), F32),
        ],
        compiler_params=pltpu.CompilerParams(
            dimension_semantics=("arbitrary", "arbitrary"),
            vmem_limit_bytes=VMEM_LIMIT_BYTES),
        name="outproj_ffn_ple",
    )(x, att, pool, p, w_out, ln_ffn.reshape(1, D), w_up, conv_w, conv_b.reshape(1, 2 * D_FF), w_down,
      ln_ple.reshape(1, D), w_ple_gate, w_ple, ln_final.reshape(1, D))


def kernel(x, p, ln_mix, w_in, pool_w, pool_scale, w_out, ln_ffn, w_up, conv_w, conv_b, w_down, ln_ple,
           w_ple_gate, w_ple, ln_final):
    depth = p.shape[0]
    slopes = jnp.exp2(-8.0 * (jnp.arange(N_ATT_HEADS, dtype=F32) + 1.0) / N_ATT_HEADS)
    assert depth == 1, "the final RMSNorm is fused into the last layer's kernel"
    i = 0
    later = (w_out[i], w_up[i], w_down[i], w_ple_gate[i], w_ple[i])
    qkv_nat, qkv_r4, qkv_r16, pool, (w_out_b, w_up_b, w_down_b, w_gate_b, w_ple_b) = _in_proj(
        x, ln_mix[i], w_in[i], pool_w[i], pool_scale[i], later, tm=1024)
    att = _attention(qkv_nat, qkv_r4, qkv_r16, slopes)
    return _ffn(x, att, pool, p[i], w_out_b, ln_ffn[i], w_up_b, conv_w[i], conv_b[i], w_down_b,
                ln_ple[i], w_gate_b, w_ple_b, ln_final, tm=512)
```

```python
import functools

import jax
import jax.numpy as jnp
from jax import lax
from jax.experimental import pallas as pl
from jax.experimental.pallas import tpu as pltpu

D_MODEL = 1024
HEAD_DIM = 64
ATT_WIDTH = 512
N_ATT_HEADS = ATT_WIDTH // HEAD_DIM
DIL_PATTERNS = ((128, 1), (512, 4), (2048, 16))
SPAN = 128
POOL_WIDTH = 512
POOL_WINDOWS = (2, 4, 8, 16)
POOL_GROUP_DIM = 128
MAX_POOL_WINDOW = 16
IN_WIDTH = 3 * ATT_WIDTH + POOL_WIDTH
D_FF = 2816
CONV_WIDTH = 3
PLE_DIM = 256
EPS = 1e-6

LANES = 128
SUBLANES = 8
HEADS_PER_SLAB = LANES // HEAD_DIM
VMEM_LIMIT_BYTES = 56 * 1024 * 1024

F32 = jnp.float32
BF16 = jnp.bfloat16
U32 = jnp.uint32


def _pack_rows(t):
    return pltpu.bitcast(t.astype(BF16), U32)


def _unpack_rows(t):
    return pltpu.bitcast(t, BF16)


def _rms_scale(t, g):
    inv = lax.rsqrt(jnp.mean(t * t, axis=-1, keepdims=True) + EPS)
    return t * inv * g


QKV_WIDTH = 3 * ATT_WIDTH
LOG2E = 1.4426950408889634
Q_SCALE = (HEAD_DIM ** -0.5) * LOG2E
N_QKV_SLABS = QKV_WIDTH // LANES


def _in_proj_body(x_ref, g_ref, w_ref, pw_ref, ps_ref, *rest, tm, n_cast):
    cast_in, rest = rest[:n_cast], rest[n_cast:]
    (nat_ref, r4_ref, r16_ref, pool_ref), rest = rest[:4], rest[4:]
    cast_out, (zbuf, z4buf, ucarry) = rest[:n_cast], rest[n_cast:]
    s_idx = pl.program_id(1)
    hw = MAX_POOL_WINDOW

    @pl.when(s_idx == 0)
    def _():
        ucarry[...] = jnp.zeros_like(ucarry)

    for src, dst in zip(cast_in, cast_out):
        dst[...] = src[...].astype(BF16)

    hn = _rms_scale(x_ref[...], g_ref[...]).astype(BF16)
    a = ATT_WIDTH

    u = jnp.dot(hn, w_ref[:, 3 * a:], preferred_element_type=F32)
    e = jnp.concatenate([ucarry[...], u], axis=0)
    ucarry[...] = u[tm - hw:, :]
    gd = POOL_GROUP_DIM
    s2 = e + pltpu.roll(e, 1, 0)
    s4 = s2[:, gd:] + pltpu.roll(s2[:, gd:], 2, 0)
    s8 = s4[:, gd:] + pltpu.roll(s4[:, gd:], 4, 0)
    s16 = s8[:, gd:] + pltpu.roll(s8[:, gd:], 8, 0)
    sums = (s2[hw:, 0:gd], s4[hw:, 0:gd], s8[hw:, 0:gd], s16[hw:, :])
    t1 = (s_idx * tm + lax.broadcasted_iota(jnp.int32, (tm, 1), 0) + 1).astype(F32)
    dlts = []
    for g, w in enumerate(POOL_WINDOWS):
        inv_cnt = 1.0 / jnp.minimum(t1, float(w))
        dlts.append((sums[g] * inv_cnt - u[:, g * gd:(g + 1) * gd]).astype(BF16))

    slabs_per_part = a // LANES
    n4 = tm // 4
    for part in range(3):
        z = jnp.dot(hn, w_ref[:, part * a:(part + 1) * a], preferred_element_type=F32)
        if part == 0:
            z = z * Q_SCALE
        nat_ref[:, part * a:(part + 1) * a] = _pack_rows(z)
        for sl in range(slabs_per_part):
            j = part * slabs_per_part + sl
            cols = slice(j * LANES, (j + 1) * LANES)
            zbuf[j] = z[:, sl * LANES:(sl + 1) * LANES]
            for c in range(4):
                cls4 = zbuf[j, pl.ds(c, n4, stride=4), :]
                r4_ref[c, :, cols] = _pack_rows(cls4)
                z4buf[j, c * n4:(c + 1) * n4, :] = cls4
            for c in range(4):
                for b in range(4):
                    cls16 = z4buf[j, pl.ds(c * n4 + b, tm // 16, stride=4), :]
                    r16_ref[4 * b + c, :, cols] = _pack_rows(cls16)

    for g in range(len(POOL_WINDOWS)):
        y = jnp.dot(dlts[g], pw_ref[g], preferred_element_type=F32)
        pool_ref[:, g * gd:(g + 1) * gd] = _pack_rows(y * ps_ref[:, g * gd:(g + 1) * gd])


def _cast_block_rows(rows, n_steps):
    br = 2 * SUBLANES
    while rows % br or rows // br > n_steps:
        br += 2 * SUBLANES
    return br


def _in_proj(x, ln_mix, w_in, pool_w, pool_scale, later_weights, *, tm):
    B, S, D = x.shape
    grid = (B, S // tm)
    n_steps = B * (S // tm)
    cast_specs = []
    for w in later_weights:
        br = _cast_block_rows(w.shape[0], n_steps)
        last = w.shape[0] // br - 1
        cast_specs.append(pl.BlockSpec(
            (br, w.shape[1]), lambda b, s, last=last: (jnp.minimum(b * (S // tm) + s, last), 0)))
    row = lambda b, s: (b, s, 0)
    cls_row = lambda b, s: (b, 0, s, 0)
    const2 = lambda b, s: (0, 0)
    const3 = lambda b, s: (0, 0, 0)
    nat, r4, r16, pool, *casted = pl.pallas_call(
        functools.partial(_in_proj_body, tm=tm, n_cast=len(later_weights)),
        grid=grid,
        in_specs=[
            pl.BlockSpec((None, tm, D), row),
            pl.BlockSpec((1, D), const2),
            pl.BlockSpec((D, IN_WIDTH), const2),
            pl.BlockSpec((len(POOL_WINDOWS), POOL_GROUP_DIM, POOL_GROUP_DIM), const3),
            pl.BlockSpec((1, POOL_WIDTH), const2),
        ] + cast_specs,
        out_specs=[
            pl.BlockSpec((None, tm // 2, QKV_WIDTH), row),
            pl.BlockSpec((None, 4, tm // 8, QKV_WIDTH), cls_row),
            pl.BlockSpec((None, 16, tm // 32, QKV_WIDTH), cls_row),
            pl.BlockSpec((None, tm // 2, POOL_WIDTH), row),
        ] + cast_specs,
        out_shape=[
            jax.ShapeDtypeStruct((B, S // 2, QKV_WIDTH), U32),
            jax.ShapeDtypeStruct((B, 4, S // 8, QKV_WIDTH), U32),
            jax.ShapeDtypeStruct((B, 16, S // 32, QKV_WIDTH), U32),
            jax.ShapeDtypeStruct((B, S // 2, POOL_WIDTH), U32),
        ] + [jax.ShapeDtypeStruct(w.shape, BF16) for w in later_weights],
        scratch_shapes=[
            pltpu.VMEM((N_QKV_SLABS, tm, LANES), F32),
            pltpu.VMEM((N_QKV_SLABS, tm, LANES), F32),
            pltpu.VMEM((MAX_POOL_WINDOW, POOL_WIDTH), F32),
        ],
        compiler_params=pltpu.CompilerParams(
            dimension_semantics=("arbitrary", "arbitrary"),
            vmem_limit_bytes=VMEM_LIMIT_BYTES),
        name="in_proj_pool",
    )(x, ln_mix.reshape(1, D), w_in.astype(BF16), pool_w.astype(BF16), pool_scale.reshape(1, POOL_WIDTH),
      *later_weights)
    return nat, r4.reshape(B, S // 2, QKV_WIDTH), r16.reshape(B, S // 2, QKV_WIDTH), pool, casted


N_BIAS_VARIANTS = 3


def _attn_body(slopes_ref, qn_ref, kn_ref, vn_ref, q4_ref, k4_ref, v4_ref, q16_ref, k16_ref, v16_ref,
               o_ref, bias_s, m_s, d_s, u_s, *, seq):
    hp = pl.program_id(1)
    n_blocks = seq // SPAN
    half = SPAN // 2
    lane = lax.broadcasted_iota(jnp.int32, (1, LANES), 1)
    is_h0 = lane < HEAD_DIM
    head_mask = (is_h0, jnp.logical_not(is_h0))
    head_sel = tuple(jnp.where(hm, 1.0, 0.0).astype(BF16) for hm in head_mask)
    ones_cols = tuple(jnp.broadcast_to(sel, (2 * SPAN, LANES)) for sel in head_sel)

    qi = lax.broadcasted_iota(jnp.int32, (SPAN, 2 * SPAN), 0)
    kj = lax.broadcasted_iota(jnp.int32, (SPAN, 2 * SPAN), 1)
    diffs = (qi - kj, qi + SPAN - kj, qi + SPAN - kj)
    extra = (None, kj >= SPAN, None)
    for p, (_, dil) in enumerate(DIL_PATTERNS):
        for hh in range(HEADS_PER_SLAB):
            slope = slopes_ref[hp * HEADS_PER_SLAB + hh]
            for var in range(N_BIAS_VARIANTS):
                diff = diffs[var]
                valid = (diff >= 0) & (diff <= SPAN)
                if extra[var] is not None:
                    valid = valid & extra[var]
                penalty = (-(slope * (float(dil) * LOG2E))) * diff.astype(F32)
                bias_s[p, var, hh] = jnp.where(valid, penalty, -jnp.inf)

    branch_refs = ((qn_ref, kn_ref, vn_ref), (q4_ref, k4_ref, v4_ref), (q16_ref, k16_ref, v16_ref))
    for p in (2, 1, 0):
        dil = DIL_PATTERNS[p][1]
        blocks_per_class = n_blocks // dil
        q_ref, k_ref, v_ref = branch_refs[p]

        for g in range(n_blocks):
            cls, jb = divmod(g, blocks_per_class)
            variant = 0 if g == 0 else (1 if jb == 0 else 2)
            row0 = g * half
            win0 = max(g - 1, 0) * half
            qb = _unpack_rows(q_ref[row0:row0 + half, :])
            kw = _unpack_rows(k_ref[win0:win0 + 2 * half, :])
            vw = _unpack_rows(v_ref[win0:win0 + 2 * half, :])
            ms, es, vs = [], [], []
            q_heads = jnp.concatenate([qb * head_sel[hh] for hh in range(HEADS_PER_SLAB)], axis=0)
            s_heads = lax.dot_general(q_heads, kw, (((1,), (1,)), ((), ())), preferred_element_type=F32)
            for hh in range(HEADS_PER_SLAB):
                s = s_heads[hh * SPAN:(hh + 1) * SPAN, :] + bias_s[p, variant, hh]
                m = jnp.max(s, axis=1, keepdims=True)
                ms.append(m)
                es.append(jnp.exp2(s - m).astype(BF16))
                vs.append(jnp.concatenate(
                    [jnp.where(head_mask[hh], vw, jnp.zeros_like(vw)), ones_cols[hh]], axis=1))
            ud = jnp.dot(jnp.concatenate(es, axis=1), jnp.concatenate(vs, axis=0),
                         preferred_element_type=F32)
            m_new = jnp.where(is_h0, ms[0], ms[1])
            u_new = ud[:, 0:LANES]
            d_new = ud[:, LANES:]
            start = (dil * SPAN) * jb + cls
            rows = pl.ds(start, SPAN) if dil == 1 else pl.ds(start, SPAN, stride=dil)
            if p == 2:
                m_s[rows, :] = m_new
                d_s[rows, :] = d_new
                u_s[rows, :] = u_new
            else:
                m_old = m_s[rows, :]
                m_all = jnp.maximum(m_old, m_new)
                w_old = jnp.exp2(m_old - m_all)
                w_new = jnp.exp2(m_new - m_all)
                d_all = d_s[rows, :] * w_old + d_new * w_new
                u_all = u_s[rows, :] * w_old + u_new * w_new
                if p == 1:
                    m_s[rows, :] = m_all
                    d_s[rows, :] = d_all
                    u_s[rows, :] = u_all
                else:
                    o_ref[row0:row0 + half, :] = _pack_rows(u_all / d_all)


def _attention(qkv_nat, qkv_r4, qkv_r16, slopes):
    B, half_s, _ = qkv_nat.shape
    S = 2 * half_s
    n_slabs = ATT_WIDTH // LANES

    def slab(part):
        return pl.BlockSpec((None, half_s, LANES), lambda b, h, slopes_ref: (b, 0, part * n_slabs + h))

    qkv_specs = [slab(0), slab(1), slab(2)]
    n_br = len(DIL_PATTERNS)
    return pl.pallas_call(
        functools.partial(_attn_body, seq=S),
        grid_spec=pltpu.PrefetchScalarGridSpec(
            num_scalar_prefetch=1,
            grid=(B, n_slabs),
            in_specs=qkv_specs * n_br,
            out_specs=pl.BlockSpec((None, half_s, LANES), lambda b, h, slopes_ref: (b, 0, h)),
            scratch_shapes=[
                pltpu.VMEM((n_br, N_BIAS_VARIANTS, HEADS_PER_SLAB, SPAN, 2 * SPAN), F32),
                pltpu.VMEM((S, LANES), F32),
                pltpu.VMEM((S, LANES), F32),
                pltpu.VMEM((S, LANES), F32),
            ],
        ),
        out_shape=jax.ShapeDtypeStruct((B, half_s, ATT_WIDTH), U32),
        compiler_params=pltpu.CompilerParams(
            dimension_semantics=("arbitrary", "arbitrary"),
            vmem_limit_bytes=VMEM_LIMIT_BYTES),
        name="dilated_attention",
    )(slopes, qkv_nat, qkv_nat, qkv_nat, qkv_r4, qkv_r4, qkv_r4, qkv_r16, qkv_r16, qkv_r16)


FF_CHUNK = 256
N_STAGE_SLOTS = 4
STAGE_PITCH = 3


def _ffn_body(x_ref, att_ref, pool_ref, p_ref, wout_ref, lnf_ref, wup_ref, cw_ref, cb_ref, wdn_ref,
              lnp_ref, wg_ref, wple_ref, lnfin_ref, o_ref, act_s, carry_s, stage_s, *, tm):
    s_idx = pl.program_id(1)

    @pl.when(s_idx == 0)
    def _():
        carry_s[...] = jnp.zeros_like(carry_s)

    th = tm // 2
    halves = (slice(0, th), slice(th, tm))
    packed = (slice(0, th // 2), slice(th // 2, tm // 2))

    hs = []
    for k in range(2):
        hk = x_ref[halves[k], :]
        hk = hk + jnp.dot(_unpack_rows(att_ref[packed[k], :]), wout_ref[0:ATT_WIDTH, :],
                          preferred_element_type=F32)
        hk = hk + jnp.dot(_unpack_rows(pool_ref[packed[k], :]), wout_ref[ATT_WIDTH:, :],
                          preferred_element_type=F32)
        hs.append(hk)
    hns = [_rms_scale(hk, lnf_ref[...]).astype(BF16) for hk in hs]

    def conv(hn, c0, slot):
        cols = slice(c0, c0 + FF_CHUNK)
        pre = jnp.dot(hn, wup_ref[:, cols], preferred_element_type=F32)
        sh1, sh2 = [], []
        sp = STAGE_PITCH
        for sl in range(FF_CHUNK // LANES):
            lanes = slice(sl * LANES, (sl + 1) * LANES)
            stage_s[slot, sl, pl.ds(0, SUBLANES, stride=sp), :] = carry_s[:, c0 + sl * LANES:c0 + (sl + 1) * LANES]
            stage_s[slot, sl, pl.ds(sp * SUBLANES, th, stride=sp), :] = pre[:, lanes]
            sh1.append(stage_s[slot, sl, pl.ds(sp * (SUBLANES - 1), th, stride=sp), :])
            sh2.append(stage_s[slot, sl, pl.ds(sp * (SUBLANES - 2), th, stride=sp), :])
        carry_s[:, cols] = pre[th - SUBLANES:, :]
        y = cb_ref[:, cols] + cw_ref[0:1, cols] * jnp.concatenate(sh2, axis=1)
        y = y + cw_ref[1:2, cols] * jnp.concatenate(sh1, axis=1)
        return y + cw_ref[2:3, cols] * pre

    n_chunks = D_FF // FF_CHUNK
    for k in range(2):
        for c in range(n_chunks):
            c0 = c * FF_CHUNK
            slot = 2 * (k * n_chunks + c)
            gate = conv(hns[k], c0, slot % N_STAGE_SLOTS)
            val = conv(hns[k], D_FF + c0, (slot + 1) % N_STAGE_SLOTS)
            act_s[halves[k], c0:c0 + FF_CHUNK] = (gate * jax.nn.sigmoid(gate) * val).astype(BF16)

    hs = [hs[k] + jnp.dot(act_s[halves[k], :], wdn_ref[...], preferred_element_type=F32) for k in range(2)]
    for k in range(2):
        hn = _rms_scale(hs[k], lnp_ref[...]).astype(BF16)
        g = jax.nn.sigmoid(jnp.dot(hn, wg_ref[...], preferred_element_type=F32))
        emb = jnp.dot(p_ref[halves[k], :].astype(BF16), wple_ref[...], preferred_element_type=F32)
        o_ref[halves[k], :] = _rms_scale(hs[k] + g * emb, lnfin_ref[...])


def _ffn(x, att, pool, p, w_out, ln_ffn, w_up, conv_w, conv_b, w_down, ln_ple, w_ple_gate, w_ple,
         ln_final, *, tm):
    B, S, D = x.shape
    grid = (B, S // tm)
    row = lambda b, s: (b, s, 0)
    const2 = lambda b, s: (0, 0)

    def resident(shape):
        return pl.BlockSpec(shape, const2, pipeline_mode=pl.Buffered(1))

    return pl.pallas_call(
        functools.partial(_ffn_body, tm=tm),
        grid=grid,
        in_specs=[
            pl.BlockSpec((None, tm, D), row),
            pl.BlockSpec((None, tm // 2, ATT_WIDTH), row),
            pl.BlockSpec((None, tm // 2, POOL_WIDTH), row),
            pl.BlockSpec((None, tm, PLE_DIM), row),
            resident((ATT_WIDTH + POOL_WIDTH, D)),
            resident((1, D)),
            resident((D, 2 * D_FF)),
            resident((CONV_WIDTH, 2 * D_FF)),
            resident((1, 2 * D_FF)),
            resident((D_FF, D)),
            resident((1, D)),
            resident((D, D)),
            resident((PLE_DIM, D)),
            resident((1, D)),
        ],
        out_specs=pl.BlockSpec((None, tm, D), row),
        out_shape=jax.ShapeDtypeStruct((B, S, D), F32),
        scratch_shapes=[
            pltpu.VMEM((tm, D_FF), BF16),
            pltpu.VMEM((SUBLANES, 2 * D_FF), F32),
            pltpu.VMEM((N_STAGE_SLOTS, FF_CHUNK // LANES, STAGE_PITCH * (SUBLANES + tm // 2), LANES), F32),
        ],
        compiler_params=pltpu.CompilerParams(
            dimension_semantics=("arbitrary", "arbitrary"),
            vmem_limit_bytes=VMEM_LIMIT_BYTES),
        name="outproj_ffn_ple",
    )(x, att, pool, p, w_out, ln_ffn.reshape(1, D), w_up, conv_w, conv_b.reshape(1, 2 * D_FF), w_down,
      ln_ple.reshape(1, D), w_ple_gate, w_ple, ln_final.reshape(1, D))


def kernel(x, p, ln_mix, w_in, pool_w, pool_scale, w_out, ln_ffn, w_up, conv_w, conv_b, w_down, ln_ple,
           w_ple_gate, w_ple, ln_final):
    depth = p.shape[0]
    slopes = jnp.exp2(-8.0 * (jnp.arange(N_ATT_HEADS, dtype=F32) + 1.0) / N_ATT_HEADS)
    assert depth == 1, "the final RMSNorm is fused into the last layer's kernel"
    i = 0
    later = (w_out[i], w_up[i], w_down[i], w_ple_gate[i], w_ple[i])
    qkv_nat, qkv_r4, qkv_r16, pool, (w_out_b, w_up_b, w_down_b, w_gate_b, w_ple_b) = _in_proj(
        x, ln_mix[i], w_in[i], pool_w[i], pool_scale[i], later, tm=1024)
    att = _attention(qkv_nat, qkv_r4, qkv_r16, slopes)
    return _ffn(x, att, pool, p[i], w_out_b, ln_ffn[i], w_up_b, conv_w[i], conv_b[i], w_down_b,
                ln_ple[i], w_gate_b, w_ple_b, ln_final, tm=512)
```

```python
import functools

import jax
import jax.numpy as jnp
from jax import lax
from jax.experimental import pallas as pl
from jax.experimental.pallas import tpu as pltpu

D_MODEL = 1024
HEAD_DIM = 64
ATT_WIDTH = 512
N_ATT_HEADS = ATT_WIDTH // HEAD_DIM
DIL_PATTERNS = ((128, 1), (512, 4), (2048, 16))
SPAN = 128
POOL_WIDTH = 512
POOL_WINDOWS = (2, 4, 8, 16)
POOL_GROUP_DIM = 128
MAX_POOL_WINDOW = 16
IN_WIDTH = 3 * ATT_WIDTH + POOL_WIDTH
D_FF = 2816
CONV_WIDTH = 3
PLE_DIM = 256
EPS = 1e-6

LANES = 128
SUBLANES = 8
HEADS_PER_SLAB = LANES // HEAD_DIM
VMEM_LIMIT_BYTES = 56 * 1024 * 1024

F32 = jnp.float32
BF16 = jnp.bfloat16
U32 = jnp.uint32


def _pack_rows(t):
    return pltpu.bitcast(t.astype(BF16), U32)


def _unpack_rows(t):
    return pltpu.bitcast(t, BF16)


def _rms_scale(t, g):
    inv = lax.rsqrt(jnp.mean(t * t, axis=-1, keepdims=True) + EPS)
    return t * inv * g


QKV_WIDTH = 3 * ATT_WIDTH
LOG2E = 1.4426950408889634
Q_SCALE = (HEAD_DIM ** -0.5) * LOG2E
N_QKV_SLABS = QKV_WIDTH // LANES


def _in_proj_body(x_ref, g_ref, w_ref, pw_ref, ps_ref, *rest, tm, n_cast):
    cast_in, rest = rest[:n_cast], rest[n_cast:]
    (nat_ref, r4_ref, r16_ref, pool_ref), rest = rest[:4], rest[4:]
    cast_out, (zbuf, z4buf, ucarry) = rest[:n_cast], rest[n_cast:]
    s_idx = pl.program_id(1)
    hw = MAX_POOL_WINDOW

    @pl.when(s_idx == 0)
    def _():
        ucarry[...] = jnp.zeros_like(ucarry)

    for src, dst in zip(cast_in, cast_out):
        dst[...] = src[...].astype(BF16)

    hn = _rms_scale(x_ref[...], g_ref[...]).astype(BF16)
    a = ATT_WIDTH

    u = jnp.dot(hn, w_ref[:, 3 * a:], preferred_element_type=F32)
    e = jnp.concatenate([ucarry[...], u], axis=0)
    ucarry[...] = u[tm - hw:, :]
    gd = POOL_GROUP_DIM
    s2 = e + pltpu.roll(e, 1, 0)
    s4 = s2[:, gd:] + pltpu.roll(s2[:, gd:], 2, 0)
    s8 = s4[:, gd:] + pltpu.roll(s4[:, gd:], 4, 0)
    s16 = s8[:, gd:] + pltpu.roll(s8[:, gd:], 8, 0)
    sums = (s2[hw:, 0:gd], s4[hw:, 0:gd], s8[hw:, 0:gd], s16[hw:, :])
    t1 = (s_idx * tm + lax.broadcasted_iota(jnp.int32, (tm, 1), 0) + 1).astype(F32)
    dlts = []
    for g, w in enumerate(POOL_WINDOWS):
        inv_cnt = 1.0 / jnp.minimum(t1, float(w))
        dlts.append((sums[g] * inv_cnt - u[:, g * gd:(g + 1) * gd]).astype(BF16))

    slabs_per_part = a // LANES
    n4 = tm // 4
    for part in range(3):
        z = jnp.dot(hn, w_ref[:, part * a:(part + 1) * a], preferred_element_type=F32)
        if part == 0:
            z = z * Q_SCALE
        nat_ref[:, part * a:(part + 1) * a] = _pack_rows(z)
        for sl in range(slabs_per_part):
            j = part * slabs_per_part + sl
            cols = slice(j * LANES, (j + 1) * LANES)
            zbuf[j] = z[:, sl * LANES:(sl + 1) * LANES]
            for c in range(4):
                cls4 = zbuf[j, pl.ds(c, n4, stride=4), :]
                r4_ref[c, :, cols] = _pack_rows(cls4)
                z4buf[j, c * n4:(c + 1) * n4, :] = cls4
            for c in range(4):
                for b in range(4):
                    cls16 = z4buf[j, pl.ds(c * n4 + b, tm // 16, stride=4), :]
                    r16_ref[4 * b + c, :, cols] = _pack_rows(cls16)

    for g in range(len(POOL_WINDOWS)):
        y = jnp.dot(dlts[g], pw_ref[g], preferred_element_type=F32)
        pool_ref[:, g * gd:(g + 1) * gd] = _pack_rows(y * ps_ref[:, g * gd:(g + 1) * gd])


def _cast_block_rows(rows, n_steps):
    br = 2 * SUBLANES
    while rows % br or rows // br > n_steps:
        br += 2 * SUBLANES
    return br


def _in_proj(x, ln_mix, w_in, pool_w, pool_scale, later_weights, *, tm):
    B, S, D = x.shape
    grid = (B, S // tm)
    n_steps = B * (S // tm)
    cast_specs = []
    for w in later_weights:
        br = _cast_block_rows(w.shape[0], n_steps)
        last = w.shape[0] // br - 1
        cast_specs.append(pl.BlockSpec(
            (br, w.shape[1]), lambda b, s, last=last: (jnp.minimum(b * (S // tm) + s, last), 0)))
    row = lambda b, s: (b, s, 0)
    cls_row = lambda b, s: (b, 0, s, 0)
    const2 = lambda b, s: (0, 0)
    const3 = lambda b, s: (0, 0, 0)
    nat, r4, r16, pool, *casted = pl.pallas_call(
        functools.partial(_in_proj_body, tm=tm, n_cast=len(later_weights)),
        grid=grid,
        in_specs=[
            pl.BlockSpec((None, tm, D), row),
            pl.BlockSpec((1, D), const2),
            pl.BlockSpec((D, IN_WIDTH), const2),
            pl.BlockSpec((len(POOL_WINDOWS), POOL_GROUP_DIM, POOL_GROUP_DIM), const3),
            pl.BlockSpec((1, POOL_WIDTH), const2),
        ] + cast_specs,
        out_specs=[
            pl.BlockSpec((None, tm // 2, QKV_WIDTH), row),
            pl.BlockSpec((None, 4, tm // 8, QKV_WIDTH), cls_row),
            pl.BlockSpec((None, 16, tm // 32, QKV_WIDTH), cls_row),
            pl.BlockSpec((None, tm // 2, POOL_WIDTH), row),
        ] + cast_specs,
        out_shape=[
            jax.ShapeDtypeStruct((B, S // 2, QKV_WIDTH), U32),
            jax.ShapeDtypeStruct((B, 4, S // 8, QKV_WIDTH), U32),
            jax.ShapeDtypeStruct((B, 16, S // 32, QKV_WIDTH), U32),
            jax.ShapeDtypeStruct((B, S // 2, POOL_WIDTH), U32),
        ] + [jax.ShapeDtypeStruct(w.shape, BF16) for w in later_weights],
        scratch_shapes=[
            pltpu.VMEM((N_QKV_SLABS, tm, LANES), F32),
            pltpu.VMEM((N_QKV_SLABS, tm, LANES), F32),
            pltpu.VMEM((MAX_POOL_WINDOW, POOL_WIDTH), F32),
        ],
        compiler_params=pltpu.CompilerParams(
            dimension_semantics=("arbitrary", "arbitrary"),
            vmem_limit_bytes=VMEM_LIMIT_BYTES),
        name="in_proj_pool",
    )(x, ln_mix.reshape(1, D), w_in.astype(BF16), pool_w.astype(BF16), pool_scale.reshape(1, POOL_WIDTH),
      *later_weights)
    return nat, r4.reshape(B, S // 2, QKV_WIDTH), r16.reshape(B, S // 2, QKV_WIDTH), pool, casted


N_BIAS_VARIANTS = 3


def _attn_body(slopes_ref, qn_ref, kn_ref, vn_ref, q4_ref, k4_ref, v4_ref, q16_ref, k16_ref, v16_ref,
               o_ref, bias_s, m_s, d_s, u_s, *, seq):
    hp = pl.program_id(1)
    n_blocks = seq // SPAN
    half = SPAN // 2
    lane = lax.broadcasted_iota(jnp.int32, (1, LANES), 1)
    is_h0 = lane < HEAD_DIM
    head_mask = (is_h0, jnp.logical_not(is_h0))
    head_sel = tuple(jnp.where(hm, 1.0, 0.0).astype(BF16) for hm in head_mask)
    ones_cols = tuple(jnp.broadcast_to(sel, (2 * SPAN, LANES)) for sel in head_sel)

    qi = lax.broadcasted_iota(jnp.int32, (SPAN, 2 * SPAN), 0)
    kj = lax.broadcasted_iota(jnp.int32, (SPAN, 2 * SPAN), 1)
    diffs = (qi - kj, qi + SPAN - kj, qi + SPAN - kj)
    extra = (None, kj >= SPAN, None)
    for p, (_, dil) in enumerate(DIL_PATTERNS):
        for hh in range(HEADS_PER_SLAB):
            slope = slopes_ref[hp * HEADS_PER_SLAB + hh]
            for var in range(N_BIAS_VARIANTS):
                diff = diffs[var]
                valid = (diff >= 0) & (diff <= SPAN)
                if extra[var] is not None:
                    valid = valid & extra[var]
                penalty = (-(slope * (float(dil) * LOG2E))) * diff.astype(F32)
                bias_s[p, var, hh] = jnp.where(valid, penalty, -jnp.inf)

    branch_refs = ((qn_ref, kn_ref, vn_ref), (q4_ref, k4_ref, v4_ref), (q16_ref, k16_ref, v16_ref))
    for p, blocks in _block_schedule(n_blocks):
        dil = DIL_PATTERNS[p][1]
        blocks_per_class = n_blocks // dil
        q_ref, k_ref, v_ref = branch_refs[p]

        for g in blocks:
            cls, jb = divmod(g, blocks_per_class)
            variant = 0 if g == 0 else (1 if jb == 0 else 2)
            row0 = g * half
            win0 = max(g - 1, 0) * half
            qb = _unpack_rows(q_ref[row0:row0 + half, :])
            kw = _unpack_rows(k_ref[win0:win0 + 2 * half, :])
            vw = _unpack_rows(v_ref[win0:win0 + 2 * half, :])
            ms, es, vs = [], [], []
            q_heads = jnp.concatenate([qb * head_sel[hh] for hh in range(HEADS_PER_SLAB)], axis=0)
            s_heads = lax.dot_general(q_heads, kw, (((1,), (1,)), ((), ())), preferred_element_type=F32)
            for hh in range(HEADS_PER_SLAB):
                s = s_heads[hh * SPAN:(hh + 1) * SPAN, :] + bias_s[p, variant, hh]
                m = jnp.max(s, axis=1, keepdims=True)
                ms.append(m)
                es.append(jnp.exp2(s - m).astype(BF16))
                vs.append(jnp.concatenate(
                    [jnp.where(head_mask[hh], vw, jnp.zeros_like(vw)), ones_cols[hh]], axis=1))
            ud = jnp.dot(jnp.concatenate(es, axis=1), jnp.concatenate(vs, axis=0),
                         preferred_element_type=F32)
            m_new = jnp.where(is_h0, ms[0], ms[1])
            u_new = ud[:, 0:LANES]
            d_new = ud[:, LANES:]
            start = (dil * SPAN) * jb + cls
            rows = pl.ds(start, SPAN) if dil == 1 else pl.ds(start, SPAN, stride=dil)
            if p == 2:
                m_s[rows, :] = m_new
                d_s[rows, :] = d_new
                u_s[rows, :] = u_new
            else:
                m_old = m_s[rows, :]
                m_all = jnp.maximum(m_old, m_new)
                w_old = jnp.exp2(m_old - m_all)
                w_new = jnp.exp2(m_new - m_all)
                d_all = d_s[rows, :] * w_old + d_new * w_new
                u_all = u_s[rows, :] * w_old + u_new * w_new
                if p == 1:
                    m_s[rows, :] = m_all
                    d_s[rows, :] = d_all
                    u_s[rows, :] = u_all
                else:
                    o_ref[row0:row0 + half, :] = _pack_rows(u_all / d_all)


def _block_schedule(n_blocks):
    n16, n4 = n_blocks // 16, n_blocks // 4
    s16 = [(4 * b + c) * n16 + jb for jb in range(n16) for c in range(4) for b in range(4)]
    s4 = [c * n4 + jb for jb in range(n4) for c in range(4)]
    s1 = list(range(n_blocks))

    def deps_ok(p, g, done):
        if p == 2:
            return True
        if p == 1:
            c, jb = divmod(g, n4)
            jb16 = (4 * jb) // 16
            return all((2, (4 * b + c) * n16 + jb16) in done for b in range(4))
        return all((1, c * n4 + g // 4) in done for c in range(4))

    streams = {2: s16, 1: s4, 0: s1}
    pos = {2: 0, 1: 0, 0: 0}
    done, order = set(), []
    while len(order) < 3 * n_blocks:
        for p in (2, 1, 0):
            if pos[p] < len(streams[p]) and deps_ok(p, streams[p][pos[p]], done):
                unit = (p, streams[p][pos[p]])
                pos[p] += 1
                done.add(unit)
                order.append((p, [unit[1]]))
    return order


def _attention(qkv_nat, qkv_r4, qkv_r16, slopes):
    B, half_s, _ = qkv_nat.shape
    S = 2 * half_s
    n_slabs = ATT_WIDTH // LANES

    def slab(part):
        return pl.BlockSpec((None, half_s, LANES), lambda b, h, slopes_ref: (b, 0, part * n_slabs + h))

    qkv_specs = [slab(0), slab(1), slab(2)]
    n_br = len(DIL_PATTERNS)
    return pl.pallas_call(
        functools.partial(_attn_body, seq=S),
        grid_spec=pltpu.PrefetchScalarGridSpec(
            num_scalar_prefetch=1,
            grid=(B, n_slabs),
            in_specs=qkv_specs * n_br,
            out_specs=pl.BlockSpec((None, half_s, LANES), lambda b, h, slopes_ref: (b, 0, h)),
            scratch_shapes=[
                pltpu.VMEM((n_br, N_BIAS_VARIANTS, HEADS_PER_SLAB, SPAN, 2 * SPAN), F32),
                pltpu.VMEM((S, LANES), F32),
                pltpu.VMEM((S, LANES), F32),
                pltpu.VMEM((S, LANES), F32),
            ],
        ),
        out_shape=jax.ShapeDtypeStruct((B, half_s, ATT_WIDTH), U32),
        compiler_params=pltpu.CompilerParams(
            dimension_semantics=("arbitrary", "arbitrary"),
            vmem_limit_bytes=VMEM_LIMIT_BYTES),
        name="dilated_attention",
    )(slopes, qkv_nat, qkv_nat, qkv_nat, qkv_r4, qkv_r4, qkv_r4, qkv_r16, qkv_r16, qkv_r16)


FF_CHUNK = 256
N_STAGE_SLOTS = 4
STAGE_PITCH = 3


def _ffn_body(x_ref, att_ref, pool_ref, p_ref, wout_ref, lnf_ref, wup_ref, cw_ref, cb_ref, wdn_ref,
              lnp_ref, wg_ref, wple_ref, lnfin_ref, o_ref, act_s, carry_s, stage_s, *, tm):
    s_idx = pl.program_id(1)

    @pl.when(s_idx == 0)
    def _():
        carry_s[...] = jnp.zeros_like(carry_s)

    th = tm // 2
    halves = (slice(0, th), slice(th, tm))
    packed = (slice(0, th // 2), slice(th // 2, tm // 2))

    hs = []
    for k in range(2):
        hk = x_ref[halves[k], :]
        hk = hk + jnp.dot(_unpack_rows(att_ref[packed[k], :]), wout_ref[0:ATT_WIDTH, :],
                          preferred_element_type=F32)
        hk = hk + jnp.dot(_unpack_rows(pool_ref[packed[k], :]), wout_ref[ATT_WIDTH:, :],
                          preferred_element_type=F32)
        hs.append(hk)
    hns = [_rms_scale(hk, lnf_ref[...]).astype(BF16) for hk in hs]

    def conv(hn, c0, slot):
        cols = slice(c0, c0 + FF_CHUNK)
        pre = jnp.dot(hn, wup_ref[:, cols], preferred_element_type=F32)
        sh1, sh2 = [], []
        sp = STAGE_PITCH
        for sl in range(FF_CHUNK // LANES):
            lanes = slice(sl * LANES, (sl + 1) * LANES)
            stage_s[slot, sl, pl.ds(0, SUBLANES, stride=sp), :] = carry_s[:, c0 + sl * LANES:c0 + (sl + 1) * LANES]
            stage_s[slot, sl, pl.ds(sp * SUBLANES, th, stride=sp), :] = pre[:, lanes]
            sh1.append(stage_s[slot, sl, pl.ds(sp * (SUBLANES - 1), th, stride=sp), :])
            sh2.append(stage_s[slot, sl, pl.ds(sp * (SUBLANES - 2), th, stride=sp), :])
        carry_s[:, cols] = pre[th - SUBLANES:, :]
        y = cb_ref[:, cols] + cw_ref[0:1, cols] * jnp.concatenate(sh2, axis=1)
        y = y + cw_ref[1:2, cols] * jnp.concatenate(sh1, axis=1)
        return y + cw_ref[2:3, cols] * pre

    n_chunks = D_FF // FF_CHUNK
    for k in range(2):
        for c in range(n_chunks):
            c0 = c * FF_CHUNK
            slot = 2 * (k * n_chunks + c)
            gate = conv(hns[k], c0, slot % N_STAGE_SLOTS)
            val = conv(hns[k], D_FF + c0, (slot + 1) % N_STAGE_SLOTS)
            act_s[halves[k], c0:c0 + FF_CHUNK] = (gate * jax.nn.sigmoid(gate) * val).astype(BF16)

    hs = [hs[k] + jnp.dot(act_s[halves[k], :], wdn_ref[...], preferred_element_type=F32) for k in range(2)]
    for k in range(2):
        hn = _rms_scale(hs[k], lnp_ref[...]).astype(BF16)
        g = jax.nn.sigmoid(jnp.dot(hn, wg_ref[...], preferred_element_type=F32))
        emb = jnp.dot(p_ref[halves[k], :].astype(BF16), wple_ref[...], preferred_element_type=F32)
        o_ref[halves[k], :] = _rms_scale(hs[k] + g * emb, lnfin_ref[...])


def _ffn(x, att, pool, p, w_out, ln_ffn, w_up, conv_w, conv_b, w_down, ln_ple, w_ple_gate, w_ple,
         ln_final, *, tm):
    B, S, D = x.shape
    grid = (B, S // tm)
    row = lambda b, s: (b, s, 0)
    const2 = lambda b, s: (0, 0)

    def resident(shape):
        return pl.BlockSpec(shape, const2, pipeline_mode=pl.Buffered(1))

    return pl.pallas_call(
        functools.partial(_ffn_body, tm=tm),
        grid=grid,
        in_specs=[
            pl.BlockSpec((None, tm, D), row),
            pl.BlockSpec((None, tm // 2, ATT_WIDTH), row),
            pl.BlockSpec((None, tm // 2, POOL_WIDTH), row),
            pl.BlockSpec((None, tm, PLE_DIM), row),
            resident((ATT_WIDTH + POOL_WIDTH, D)),
            resident((1, D)),
            resident((D, 2 * D_FF)),
            resident((CONV_WIDTH, 2 * D_FF)),
            resident((1, 2 * D_FF)),
            resident((D_FF, D)),
            resident((1, D)),
            resident((D, D)),
            resident((PLE_DIM, D)),
            resident((1, D)),
        ],
        out_specs=pl.BlockSpec((None, tm, D), row),
        out_shape=jax.ShapeDtypeStruct((B, S, D), F32),
        scratch_shapes=[
            pltpu.VMEM((tm, D_FF), BF16),
            pltpu.VMEM((SUBLANES, 2 * D_FF), F32),
            pltpu.VMEM((N_STAGE_SLOTS, FF_CHUNK // LANES, STAGE_PITCH * (SUBLANES + tm // 2), LANES), F32),
        ],
        compiler_params=pltpu.CompilerParams(
            dimension_semantics=("arbitrary", "arbitrary"),
            vmem_limit_bytes=VMEM_LIMIT_BYTES),
        name="outproj_ffn_ple",
    )(x, att, pool, p, w_out, ln_ffn.reshape(1, D), w_up, conv_w, conv_b.reshape(1, 2 * D_FF), w_down,
      ln_ple.reshape(1, D), w_ple_gate, w_ple, ln_final.reshape(1, D))


def kernel(x, p, ln_mix, w_in, pool_w, pool_scale, w_out, ln_ffn, w_up, conv_w, conv_b, w_down, ln_ple,
           w_ple_gate, w_ple, ln_final):
    depth = p.shape[0]
    slopes = jnp.exp2(-8.0 * (jnp.arange(N_ATT_HEADS, dtype=F32) + 1.0) / N_ATT_HEADS)
    assert depth == 1, "the final RMSNorm is fused into the last layer's kernel"
    i = 0
    later = (w_out[i], w_up[i], w_down[i], w_ple_gate[i], w_ple[i])
    qkv_nat, qkv_r4, qkv_r16, pool, (w_out_b, w_up_b, w_down_b, w_gate_b, w_ple_b) = _in_proj(
        x, ln_mix[i], w_in[i], pool_w[i], pool_scale[i], later, tm=1024)
    att = _attention(qkv_nat, qkv_r4, qkv_r16, slopes)
    return _ffn(x, att, pool, p[i], w_out_b, ln_ffn[i], w_up_b, conv_w[i], conv_b[i], w_down_b,
                ln_ple[i], w_gate_b, w_ple_b, ln_final, tm=512)
```

```python
import functools

import jax
import jax.numpy as jnp
from jax import lax
from jax.experimental import pallas as pl
from jax.experimental.pallas import tpu as pltpu

D_MODEL = 1024
HEAD_DIM = 64
ATT_WIDTH = 512
N_ATT_HEADS = ATT_WIDTH // HEAD_DIM
DIL_PATTERNS = ((128, 1), (512, 4), (2048, 16))
SPAN = 128
POOL_WIDTH = 512
POOL_WINDOWS = (2, 4, 8, 16)
POOL_GROUP_DIM = 128
MAX_POOL_WINDOW = 16
IN_WIDTH = 3 * ATT_WIDTH + POOL_WIDTH
D_FF = 2816
CONV_WIDTH = 3
PLE_DIM = 256
EPS = 1e-6

LANES = 128
SUBLANES = 8
HEADS_PER_SLAB = LANES // HEAD_DIM
VMEM_LIMIT_BYTES = 56 * 1024 * 1024

F32 = jnp.float32
BF16 = jnp.bfloat16
U32 = jnp.uint32


def _pack_rows(t):
    return pltpu.bitcast(t.astype(BF16), U32)


def _unpack_rows(t):
    return pltpu.bitcast(t, BF16)


def _rms_scale(t, g):
    inv = lax.rsqrt(jnp.mean(t * t, axis=-1, keepdims=True) + EPS)
    return t * inv * g


QKV_WIDTH = 3 * ATT_WIDTH
LOG2E = 1.4426950408889634
Q_SCALE = (HEAD_DIM ** -0.5) * LOG2E
N_QKV_SLABS = QKV_WIDTH // LANES


def _in_proj_body(x_ref, g_ref, w_ref, pw_ref, ps_ref, *rest, tm, n_cast):
    cast_in, rest = rest[:n_cast], rest[n_cast:]
    (nat_ref, r4_ref, r16_ref, pool_ref), rest = rest[:4], rest[4:]
    cast_out, (zbuf, z4buf, ucarry) = rest[:n_cast], rest[n_cast:]
    s_idx = pl.program_id(1)
    hw = MAX_POOL_WINDOW

    @pl.when(s_idx == 0)
    def _():
        ucarry[...] = jnp.zeros_like(ucarry)

    for src, dst in zip(cast_in, cast_out):
        dst[...] = src[...].astype(BF16)

    hn = _rms_scale(x_ref[...], g_ref[...]).astype(BF16)
    a = ATT_WIDTH

    u = jnp.dot(hn, w_ref[:, 3 * a:], preferred_element_type=F32)
    e = jnp.concatenate([ucarry[...], u], axis=0)
    ucarry[...] = u[tm - hw:, :]
    gd = POOL_GROUP_DIM
    s2 = e + pltpu.roll(e, 1, 0)
    s4 = s2[:, gd:] + pltpu.roll(s2[:, gd:], 2, 0)
    s8 = s4[:, gd:] + pltpu.roll(s4[:, gd:], 4, 0)
    s16 = s8[:, gd:] + pltpu.roll(s8[:, gd:], 8, 0)
    sums = (s2[hw:, 0:gd], s4[hw:, 0:gd], s8[hw:, 0:gd], s16[hw:, :])
    t1 = (s_idx * tm + lax.broadcasted_iota(jnp.int32, (tm, 1), 0) + 1).astype(F32)
    dlts = []
    for g, w in enumerate(POOL_WINDOWS):
        inv_cnt = 1.0 / jnp.minimum(t1, float(w))
        dlts.append((sums[g] * inv_cnt - u[:, g * gd:(g + 1) * gd]).astype(BF16))

    slabs_per_part = a // LANES
    n4 = tm // 4
    for part in range(3):
        z = jnp.dot(hn, w_ref[:, part * a:(part + 1) * a], preferred_element_type=F32)
        if part == 0:
            z = z * Q_SCALE
        nat_ref[:, part * a:(part + 1) * a] = _pack_rows(z)
        for sl in range(slabs_per_part):
            j = part * slabs_per_part + sl
            cols = slice(j * LANES, (j + 1) * LANES)
            zbuf[j] = z[:, sl * LANES:(sl + 1) * LANES]
            for c in range(4):
                cls4 = zbuf[j, pl.ds(c, n4, stride=4), :]
                r4_ref[c, :, cols] = _pack_rows(cls4)
                z4buf[j, c * n4:(c + 1) * n4, :] = cls4
            for c in range(4):
                for b in range(4):
                    cls16 = z4buf[j, pl.ds(c * n4 + b, tm // 16, stride=4), :]
                    r16_ref[4 * b + c, :, cols] = _pack_rows(cls16)

    for g in range(len(POOL_WINDOWS)):
        y = jnp.dot(dlts[g], pw_ref[g], preferred_element_type=F32)
        pool_ref[:, g * gd:(g + 1) * gd] = _pack_rows(y * ps_ref[:, g * gd:(g + 1) * gd])


def _cast_block_rows(rows, n_steps):
    br = 2 * SUBLANES
    while rows % br or rows // br > n_steps:
        br += 2 * SUBLANES
    return br


def _in_proj(x, ln_mix, w_in, pool_w, pool_scale, later_weights, *, tm):
    B, S, D = x.shape
    grid = (B, S // tm)
    n_steps = B * (S // tm)
    cast_specs = []
    for w in later_weights:
        br = _cast_block_rows(w.shape[0], n_steps)
        last = w.shape[0] // br - 1
        cast_specs.append(pl.BlockSpec(
            (br, w.shape[1]), lambda b, s, last=last: (jnp.minimum(b * (S // tm) + s, last), 0)))
    row = lambda b, s: (b, s, 0)
    cls_row = lambda b, s: (b, 0, s, 0)
    const2 = lambda b, s: (0, 0)
    const3 = lambda b, s: (0, 0, 0)
    nat, r4, r16, pool, *casted = pl.pallas_call(
        functools.partial(_in_proj_body, tm=tm, n_cast=len(later_weights)),
        grid=grid,
        in_specs=[
            pl.BlockSpec((None, tm, D), row),
            pl.BlockSpec((1, D), const2),
            pl.BlockSpec((D, IN_WIDTH), const2),
            pl.BlockSpec((len(POOL_WINDOWS), POOL_GROUP_DIM, POOL_GROUP_DIM), const3),
            pl.BlockSpec((1, POOL_WIDTH), const2),
        ] + cast_specs,
        out_specs=[
            pl.BlockSpec((None, tm // 2, QKV_WIDTH), row),
            pl.BlockSpec((None, 4, tm // 8, QKV_WIDTH), cls_row),
            pl.BlockSpec((None, 16, tm // 32, QKV_WIDTH), cls_row),
            pl.BlockSpec((None, tm // 2, POOL_WIDTH), row),
        ] + cast_specs,
        out_shape=[
            jax.ShapeDtypeStruct((B, S // 2, QKV_WIDTH), U32),
            jax.ShapeDtypeStruct((B, 4, S // 8, QKV_WIDTH), U32),
            jax.ShapeDtypeStruct((B, 16, S // 32, QKV_WIDTH), U32),
            jax.ShapeDtypeStruct((B, S // 2, POOL_WIDTH), U32),
        ] + [jax.ShapeDtypeStruct(w.shape, BF16) for w in later_weights],
        scratch_shapes=[
            pltpu.VMEM((N_QKV_SLABS, tm, LANES), F32),
            pltpu.VMEM((N_QKV_SLABS, tm, LANES), F32),
            pltpu.VMEM((MAX_POOL_WINDOW, POOL_WIDTH), F32),
        ],
        compiler_params=pltpu.CompilerParams(
            dimension_semantics=("arbitrary", "arbitrary"),
            vmem_limit_bytes=VMEM_LIMIT_BYTES),
        name="in_proj_pool",
    )(x, ln_mix.reshape(1, D), w_in.astype(BF16), pool_w.astype(BF16), pool_scale.reshape(1, POOL_WIDTH),
      *later_weights)
    return nat, r4.reshape(B, S // 2, QKV_WIDTH), r16.reshape(B, S // 2, QKV_WIDTH), pool, casted


N_BIAS_VARIANTS = 3


def _attn_body(slopes_ref, qn_ref, kn_ref, vn_ref, q4_ref, k4_ref, v4_ref, q16_ref, k16_ref, v16_ref,
               o_ref, bias_s, m_s, d_s, u_s, *, seq):
    hp = pl.program_id(1)
    n_blocks = seq // SPAN
    half = SPAN // 2
    lane = lax.broadcasted_iota(jnp.int32, (1, LANES), 1)
    is_h0 = lane < HEAD_DIM
    head_mask = (is_h0, jnp.logical_not(is_h0))
    head_sel = tuple(jnp.where(hm, 1.0, 0.0).astype(BF16) for hm in head_mask)
    ones_cols = tuple(jnp.broadcast_to(sel, (2 * SPAN, LANES)) for sel in head_sel)

    qi = lax.broadcasted_iota(jnp.int32, (SPAN, 2 * SPAN), 0)
    kj = lax.broadcasted_iota(jnp.int32, (SPAN, 2 * SPAN), 1)
    diffs = (qi - kj, qi + SPAN - kj, qi + SPAN - kj)
    extra = (None, kj >= SPAN, None)
    for p, (_, dil) in enumerate(DIL_PATTERNS):
        for hh in range(HEADS_PER_SLAB):
            slope = slopes_ref[hp * HEADS_PER_SLAB + hh]
            for var in range(N_BIAS_VARIANTS):
                diff = diffs[var]
                valid = (diff >= 0) & (diff <= SPAN)
                if extra[var] is not None:
                    valid = valid & extra[var]
                penalty = (-(slope * (float(dil) * LOG2E))) * diff.astype(F32)
                bias_s[p, var, hh] = jnp.where(valid, penalty, -jnp.inf)

    branch_refs = ((qn_ref, kn_ref, vn_ref), (q4_ref, k4_ref, v4_ref), (q16_ref, k16_ref, v16_ref))
    for p in (2, 1, 0):
        dil = DIL_PATTERNS[p][1]
        blocks_per_class = n_blocks // dil
        q_ref, k_ref, v_ref = branch_refs[p]

        for g in range(n_blocks):
            cls, jb = divmod(g, blocks_per_class)
            variant = 0 if g == 0 else (1 if jb == 0 else 2)
            row0 = g * half
            win0 = max(g - 1, 0) * half
            qb = _unpack_rows(q_ref[row0:row0 + half, :])
            kw = _unpack_rows(k_ref[win0:win0 + 2 * half, :])
            vw = _unpack_rows(v_ref[win0:win0 + 2 * half, :])
            ms, es, vs = [], [], []
            q_heads = jnp.concatenate([qb * head_sel[hh] for hh in range(HEADS_PER_SLAB)], axis=0)
            s_heads = lax.dot_general(q_heads, kw, (((1,), (1,)), ((), ())), preferred_element_type=F32)
            for hh in range(HEADS_PER_SLAB):
                s = s_heads[hh * SPAN:(hh + 1) * SPAN, :] + bias_s[p, variant, hh]
                m = jnp.max(s, axis=1, keepdims=True)
                ms.append(m)
                es.append(jnp.exp2(s - m).astype(BF16))
                vs.append(jnp.concatenate(
                    [jnp.where(head_mask[hh], vw, jnp.zeros_like(vw)), ones_cols[hh]], axis=1))
            ud = jnp.dot(jnp.concatenate(es, axis=1), jnp.concatenate(vs, axis=0),
                         preferred_element_type=F32)
            m_new = jnp.where(is_h0, ms[0], ms[1])
            u_new = ud[:, 0:LANES]
            d_new = ud[:, LANES:]
            start = (dil * SPAN) * jb + cls
            rows = pl.ds(start, SPAN) if dil == 1 else pl.ds(start, SPAN, stride=dil)
            if p == 2:
                m_s[rows, :] = m_new
                d_s[rows, :] = d_new
                u_s[rows, :] = u_new
            else:
                m_old = m_s[rows, :]
                m_all = jnp.maximum(m_old, m_new)
                w_old = jnp.exp2(m_old - m_all)
                w_new = jnp.exp2(m_new - m_all)
                d_all = d_s[rows, :] * w_old + d_new * w_new
                u_all = u_s[rows, :] * w_old + u_new * w_new
                if p == 1:
                    m_s[rows, :] = m_all
                    d_s[rows, :] = d_all
                    u_s[rows, :] = u_all
                else:
                    o_ref[row0:row0 + half, :] = _pack_rows(u_all / d_all)


def _attention(qkv_nat, qkv_r4, qkv_r16, slopes):
    B, half_s, _ = qkv_nat.shape
    S = 2 * half_s
    n_slabs = ATT_WIDTH // LANES

    def slab(part):
        return pl.BlockSpec((None, half_s, LANES), lambda b, h, slopes_ref: (b, 0, part * n_slabs + h))

    qkv_specs = [slab(0), slab(1), slab(2)]
    n_br = len(DIL_PATTERNS)
    return pl.pallas_call(
        functools.partial(_attn_body, seq=S),
        grid_spec=pltpu.PrefetchScalarGridSpec(
            num_scalar_prefetch=1,
            grid=(B, n_slabs),
            in_specs=qkv_specs * n_br,
            out_specs=pl.BlockSpec((None, half_s, LANES), lambda b, h, slopes_ref: (b, 0, h)),
            scratch_shapes=[
                pltpu.VMEM((n_br, N_BIAS_VARIANTS, HEADS_PER_SLAB, SPAN, 2 * SPAN), F32),
                pltpu.VMEM((S, LANES), F32),
                pltpu.VMEM((S, LANES), F32),
                pltpu.VMEM((S, LANES), F32),
            ],
        ),
        out_shape=jax.ShapeDtypeStruct((B, half_s, ATT_WIDTH), U32),
        compiler_params=pltpu.CompilerParams(
            dimension_semantics=("arbitrary", "arbitrary"),
            vmem_limit_bytes=VMEM_LIMIT_BYTES),
        name="dilated_attention",
    )(slopes, qkv_nat, qkv_nat, qkv_nat, qkv_r4, qkv_r4, qkv_r4, qkv_r16, qkv_r16, qkv_r16)


FF_CHUNK = 256
N_STAGE_SLOTS = 4
STAGE_PITCH = 3


def _ffn_body(x_ref, att_ref, pool_ref, p_ref, wout_ref, lnf_ref, wup_ref, cw_ref, cb_ref, wdn_ref,
              lnp_ref, wg_ref, wple_ref, lnfin_ref, o_ref, act_s, carry_s, stage_s, *, tm):
    s_idx = pl.program_id(1)

    @pl.when(s_idx == 0)
    def _():
        carry_s[...] = jnp.zeros_like(carry_s)

    th = tm // 2
    halves = (slice(0, th), slice(th, tm))
    packed = (slice(0, th // 2), slice(th // 2, tm // 2))

    hs = []
    for k in range(2):
        hk = x_ref[halves[k], :]
        hk = hk + jnp.dot(_unpack_rows(att_ref[packed[k], :]), wout_ref[0:ATT_WIDTH, :],
                          preferred_element_type=F32)
        hk = hk + jnp.dot(_unpack_rows(pool_ref[packed[k], :]), wout_ref[ATT_WIDTH:, :],
                          preferred_element_type=F32)
        hs.append(hk)
    hns = [_rms_scale(hk, lnf_ref[...]).astype(BF16) for hk in hs]

    def conv(hn, c0, slot):
        cols = slice(c0, c0 + FF_CHUNK)
        pre = jnp.dot(hn, wup_ref[:, cols], preferred_element_type=F32)
        sh1, sh2 = [], []
        sp = STAGE_PITCH
        for sl in range(FF_CHUNK // LANES):
            lanes = slice(sl * LANES, (sl + 1) * LANES)
            stage_s[slot, sl, pl.ds(0, SUBLANES, stride=sp), :] = carry_s[:, c0 + sl * LANES:c0 + (sl + 1) * LANES]
            stage_s[slot, sl, pl.ds(sp * SUBLANES, th, stride=sp), :] = pre[:, lanes]
            sh1.append(stage_s[slot, sl, pl.ds(sp * (SUBLANES - 1), th, stride=sp), :])
            sh2.append(stage_s[slot, sl, pl.ds(sp * (SUBLANES - 2), th, stride=sp), :])
        carry_s[:, cols] = pre[th - SUBLANES:, :]
        y = cb_ref[:, cols] + cw_ref[0:1, cols] * jnp.concatenate(sh2, axis=1)
        y = y + cw_ref[1:2, cols] * jnp.concatenate(sh1, axis=1)
        return y + cw_ref[2:3, cols] * pre

    n_chunks = D_FF // FF_CHUNK
    for k in range(2):
        for c in range(n_chunks):
            c0 = c * FF_CHUNK
            slot = 2 * (k * n_chunks + c)
            gate = conv(hns[k], c0, slot % N_STAGE_SLOTS)
            val = conv(hns[k], D_FF + c0, (slot + 1) % N_STAGE_SLOTS)
            hg = 0.5 * gate
            act_s[halves[k], c0:c0 + FF_CHUNK] = (hg * (1.0 + jnp.tanh(hg)) * val).astype(BF16)

    hs = [hs[k] + jnp.dot(act_s[halves[k], :], wdn_ref[...], preferred_element_type=F32) for k in range(2)]
    for k in range(2):
        hn = _rms_scale(hs[k], lnp_ref[...]).astype(BF16)
        g = jax.nn.sigmoid(jnp.dot(hn, wg_ref[...], preferred_element_type=F32))
        emb = jnp.dot(p_ref[halves[k], :].astype(BF16), wple_ref[...], preferred_element_type=F32)
        o_ref[halves[k], :] = _rms_scale(hs[k] + g * emb, lnfin_ref[...])


def _ffn(x, att, pool, p, w_out, ln_ffn, w_up, conv_w, conv_b, w_down, ln_ple, w_ple_gate, w_ple,
         ln_final, *, tm):
    B, S, D = x.shape
    grid = (B, S // tm)
    row = lambda b, s: (b, s, 0)
    const2 = lambda b, s: (0, 0)

    def resident(shape):
        return pl.BlockSpec(shape, const2, pipeline_mode=pl.Buffered(1))

    return pl.pallas_call(
        functools.partial(_ffn_body, tm=tm),
        grid=grid,
        in_specs=[
            pl.BlockSpec((None, tm, D), row),
            pl.BlockSpec((None, tm // 2, ATT_WIDTH), row),
            pl.BlockSpec((None, tm // 2, POOL_WIDTH), row),
            pl.BlockSpec((None, tm, PLE_DIM), row),
            resident((ATT_WIDTH + POOL_WIDTH, D)),
            resident((1, D)),
            resident((D, 2 * D_FF)),
            resident((CONV_WIDTH, 2 * D_FF)),
            resident((1, 2 * D_FF)),
            resident((D_FF, D)),
            resident((1, D)),
            resident((D, D)),
            resident((PLE_DIM, D)),
            resident((1, D)),
        ],
        out_specs=pl.BlockSpec((None, tm, D), row),
        out_shape=jax.ShapeDtypeStruct((B, S, D), F32),
        scratch_shapes=[
            pltpu.VMEM((tm, D_FF), BF16),
            pltpu.VMEM((SUBLANES, 2 * D_FF), F32),
            pltpu.VMEM((N_STAGE_SLOTS, FF_CHUNK // LANES, STAGE_PITCH * (SUBLANES + tm // 2), LANES), F32),
        ],
        compiler_params=pltpu.CompilerParams(
            dimension_semantics=("arbitrary", "arbitrary"),
            vmem_limit_bytes=VMEM_LIMIT_BYTES),
        name="outproj_ffn_ple",
    )(x, att, pool, p, w_out, ln_ffn.reshape(1, D), w_up, conv_w, conv_b.reshape(1, 2 * D_FF), w_down,
      ln_ple.reshape(1, D), w_ple_gate, w_ple, ln_final.reshape(1, D))


def kernel(x, p, ln_mix, w_in, pool_w, pool_scale, w_out, ln_ffn, w_up, conv_w, conv_b, w_down, ln_ple,
           w_ple_gate, w_ple, ln_final):
    depth = p.shape[0]
    slopes = jnp.exp2(-8.0 * (jnp.arange(N_ATT_HEADS, dtype=F32) + 1.0) / N_ATT_HEADS)
    assert depth == 1, "the final RMSNorm is fused into the last layer's kernel"
    i = 0
    later = (w_out[i], w_up[i], w_down[i], w_ple_gate[i], w_ple[i])
    qkv_nat, qkv_r4, qkv_r16, pool, (w_out_b, w_up_b, w_down_b, w_gate_b, w_ple_b) = _in_proj(
        x, ln_mix[i], w_in[i], pool_w[i], pool_scale[i], later, tm=1024)
    att = _attention(qkv_nat, qkv_r4, qkv_r16, slopes)
    return _ffn(x, att, pool, p[i], w_out_b, ln_ffn[i], w_up_b, conv_w[i], conv_b[i], w_down_b,
                ln_ple[i], w_gate_b, w_ple_b, ln_final, tm=512)
```
